```python
import math, functools
import jax, jax.numpy as jnp
from jax import lax
import numpy as np

D_MODEL = 1024
BATCH = 4
SEQ = 4096
DEPTH = 4
DEC_BATCH = 32
DEC_SEQ = 1
PAST_LEN = 8192
PAGE_SIZE = 128

D_MIX = D_MODEL
D_ATT = D_MIX // 2
HEAD_DIM = 64
N_HEADS = D_ATT // HEAD_DIM
N_IDX_HEADS = 4
D_IDX = 64
TOPK_MAX = 256
Q_BLOCK = 128
N_BUCKETS = 32
MAX_DISTANCE = 128
D_SSM = D_MIX // 4
SSM_CH = 16
N_SSM_GROUPS = D_SSM // SSM_CH
SSM_STATE = 64
D_LRU = D_MIX - D_ATT - D_SSM
N_LRU_BLOCKS = 4
LRU_BLOCK = D_LRU // N_LRU_BLOCKS
CONV_W = 4
LRU_C = 8.0
LN_EPS = 1e-5
DEEPNORM_ALPHA = (2.0 * DEPTH) ** 0.25
DEEPNORM_BETA = (8.0 * DEPTH) ** -0.25
D_IN = 4 * D_ATT + N_IDX_HEADS * D_IDX + D_IDX + N_IDX_HEADS + 2 * D_SSM + 2 * D_LRU

kernel_name = "hymba_dsa_s5_rglru_decode_step"


def layer_norm(x, g, b):
    xf = x.astype(jnp.float32)
    mu = jnp.mean(xf, axis=-1, keepdims=True)
    var = jnp.mean(jnp.square(xf - mu), axis=-1, keepdims=True)
    return ((xf - mu) * lax.rsqrt(var + LN_EPS) * g.astype(jnp.float32) + b.astype(jnp.float32)).astype(x.dtype)


def project(x, w_in_l):
    widths = (D_ATT, D_ATT, D_ATT, D_ATT, N_IDX_HEADS * D_IDX, D_IDX, N_IDX_HEADS,
              D_SSM, D_SSM, D_LRU, D_LRU)
    cuts = np.cumsum(widths)[:-1].tolist()
    h = jnp.einsum('btd,de->bte', x, w_in_l)
    return jnp.split(h, cuts, axis=-1)


def rel_bucket(dist):
    dist = jnp.maximum(dist, 0)
    exact = N_BUCKETS // 2
    far = exact + (jnp.log(jnp.maximum(dist, exact).astype(jnp.float32) / exact)
                   / math.log(MAX_DISTANCE / exact) * (N_BUCKETS - exact)).astype(jnp.int32)
    return jnp.where(dist < exact, dist, jnp.minimum(far, N_BUCKETS - 1))


def index_scores(q_idx, w_idx, k_idx):
    dots = jnp.einsum('bqhd,bsd->bqhs', q_idx.astype(jnp.float32), k_idx.astype(jnp.float32)) * (D_IDX ** -0.5)
    return jnp.einsum('bqh,bqhs->bqs', w_idx.astype(jnp.float32) * (N_IDX_HEADS ** -0.5), jax.nn.relu(dots))


def select_keys(scores, q_pos, k_pos, kk):
    admissible = k_pos[None, None, :] <= q_pos[None, :, None]
    vals, idx = lax.top_k(jnp.where(admissible, scores, -jnp.inf), kk)
    return idx, jnp.isfinite(vals)


def sparse_attention(q, k_sel, v_sel, sel_pos, q_pos, valid, rel_bias):
    logits = jnp.einsum('bqhd,bqkhd->bqkh', q.astype(jnp.float32), k_sel.astype(jnp.float32)) * (HEAD_DIM ** -0.5)
    logits = logits + rel_bias.astype(jnp.float32)[rel_bucket(q_pos[None, :, None] - sel_pos)]
    logits = jnp.where(valid[..., None], logits, -jnp.inf)
    p = jax.nn.softmax(logits, axis=2)
    return jnp.einsum('bqkh,bqkhd->bqhd', p, v_sel.astype(jnp.float32)).astype(q.dtype)


def gather_rows(a, i):
    return jax.vmap(lambda ab, ib: ab[ib])(a, i)


def prompt_attention(q, k, v, q_idx, k_idx, w_idx, rel_bias):
    bsz, t = q.shape[0], q.shape[1]
    kk = min(TOPK_MAX, t // 4)
    nb = t // Q_BLOCK
    k_pos = jnp.arange(t, dtype=jnp.int32)

    def to_blocks(a):
        return jnp.swapaxes(a.reshape((bsz, nb, Q_BLOCK) + a.shape[2:]), 0, 1)

    def block(args):
        qb, qib, wb, start = args
        q_pos = start + jnp.arange(Q_BLOCK, dtype=jnp.int32)
        idx, valid = select_keys(index_scores(qib, wb, k_idx), q_pos, k_pos, kk)
        return sparse_attention(qb, gather_rows(k, idx), gather_rows(v, idx), idx, q_pos, valid, rel_bias)

    starts = jnp.arange(nb, dtype=jnp.int32) * Q_BLOCK
    o = lax.map(block, (to_blocks(q), to_blocks(q_idx), to_blocks(w_idx), starts))
    return jnp.swapaxes(o, 0, 1).reshape(bsz, t, N_HEADS, HEAD_DIM)


def sample_attention(q, k, v, q_idx, k_idx, w_idx, *, cache_k, cache_v, cache_kidx, layer, page_table, rel_bias):
    bsz, s = q.shape[0], q.shape[1]
    n_pages = page_table.shape[1]
    page = cache_k.shape[2]
    past = n_pages * page
    total = past + s
    kk = min(TOPK_MAX, total // 4)
    past_kidx = cache_kidx[layer, page_table].reshape(bsz, past, D_IDX)
    k_idx_all = jnp.concatenate([past_kidx.astype(k_idx.dtype), k_idx], axis=1)
    q_pos = past + jnp.arange(s, dtype=jnp.int32)
    k_pos = jnp.arange(total, dtype=jnp.int32)
    idx, valid = select_keys(index_scores(q_idx, w_idx, k_idx_all), q_pos, k_pos, kk)
    in_past = (idx < past)[..., None, None]
    phys = page_table[jnp.arange(bsz)[:, None, None], jnp.minimum(idx // page, n_pages - 1)]
    off = idx % page
    new_i = jnp.clip(idx - past, 0, s - 1)
    k_sel = jnp.where(in_past, cache_k[layer, phys, off].astype(k.dtype), gather_rows(k, new_i))
    v_sel = jnp.where(in_past, cache_v[layer, phys, off].astype(v.dtype), gather_rows(v, new_i))
    return sparse_attention(q, k_sel, v_sel, idx, q_pos, valid, rel_bias)


def linear_scan(a, b, h0):
    b = b.at[:, 0].add(a[:, 0] * h0)
    _, h = lax.associative_scan(lambda l, r: (l[0] * r[0], r[0] * l[1] + r[1]), (a, b), axis=1)
    return h


def complex_linear_scan(a_re, a_im, b_re, b_im, h_re0, h_im0):
    b_re = b_re.at[:, 0].add(a_re * h_re0 - a_im * h_im0)
    b_im = b_im.at[:, 0].add(a_re * h_im0 + a_im * h_re0)
    ar = jnp.broadcast_to(a_re, b_re.shape)
    ai = jnp.broadcast_to(a_im, b_re.shape)

    def comb(l, r):
        ar1, ai1, br1, bi1 = l
        ar2, ai2, br2, bi2 = r
        return (ar1 * ar2 - ai1 * ai2, ar1 * ai2 + ai1 * ar2,
                ar2 * br1 - ai2 * bi1 + br2, ar2 * bi1 + ai2 * br1 + bi2)

    _, _, h_re, h_im = lax.associative_scan(comb, (ar, ai, b_re, b_im), axis=1)
    return h_re, h_im


def s5_discretize(lam_re, lam_im, log_step, b_re, b_im):
    f32 = jnp.float32
    lr = jnp.minimum(lam_re.astype(f32), -1e-4)
    li = lam_im.astype(f32)
    dt = jnp.exp(log_step.astype(f32))[:, None]
    mag = jnp.exp(lr * dt)
    a_re = mag * jnp.cos(li * dt)
    a_im = mag * jnp.sin(li * dt)
    den = lr * lr + li * li
    f_re = ((a_re - 1.0) * lr + a_im * li) / den
    f_im = (a_im * lr - (a_re - 1.0) * li) / den
    br, bi = b_re.astype(f32), b_im.astype(f32)
    bb_re = f_re[..., None] * br - f_im[..., None] * bi
    bb_im = f_re[..., None] * bi + f_im[..., None] * br
    return a_re, a_im, bb_re, bb_im


def s5_branch(u, gate, h_re0, h_im0, lp):
    f32 = jnp.float32
    bsz, t = u.shape[0], u.shape[1]
    uf = u.astype(f32)
    a_re, a_im, bb_re, bb_im = s5_discretize(lp['lam_re'], lp['lam_im'], lp['log_step'], lp['b_re'], lp['b_im'])
    ug = uf.reshape(bsz, t, N_SSM_GROUPS, SSM_CH)
    bu_re = jnp.einsum('btgc,gpc->btgp', ug, bb_re)
    bu_im = jnp.einsum('btgc,gpc->btgp', ug, bb_im)
    h_re, h_im = complex_linear_scan(a_re, a_im, bu_re, bu_im, h_re0.astype(f32), h_im0.astype(f32))
    y = (jnp.einsum('btgp,gcp->btgc', h_re, lp['c_re'].astype(f32))
         - jnp.einsum('btgp,gcp->btgc', h_im, lp['c_im'].astype(f32)))
    y = y.reshape(bsz, t, D_SSM) + lp['d_skip'].astype(f32) * uf
    z = jax.nn.gelu(y)
    z = z * jax.nn.sigmoid(z @ lp['glu_w'].astype(f32) + lp['glu_b'].astype(f32))
    return z * jax.nn.silu(gate.astype(f32)), h_re[:, -1], h_im[:, -1]


def rglru_branch(u, gate, conv0, h0, lp):
    f32 = jnp.float32
    bsz, t = u.shape[0], u.shape[1]
    ext = jnp.concatenate([conv0.astype(f32), u.astype(f32)], axis=1)
    w = lp['conv_w'].astype(f32)
    xc = lp['conv_b'].astype(f32) + ext[:, 0:t] * w[0]
    for j in range(1, CONV_W):
        xc = xc + ext[:, j:j + t] * w[j]
    xb = xc.reshape(bsz, t, N_LRU_BLOCKS, LRU_BLOCK)
    r = jax.nn.sigmoid(jnp.einsum('btnd,nde->btne', xb, lp['wa'].astype(f32)).reshape(bsz, t, D_LRU) + lp['ba'].astype(f32))
    i = jax.nn.sigmoid(jnp.einsum('btnd,nde->btne', xb, lp['wx'].astype(f32)).reshape(bsz, t, D_LRU) + lp['bx'].astype(f32))
    log_a = -LRU_C * r * jax.nn.softplus(-lp['lam'].astype(f32))
    a = jnp.exp(log_a)
    b = jnp.sqrt(-jnp.expm1(2.0 * log_a)) * (i * xc)
    h = linear_scan(a, b, h0.astype(f32))
    return h * jax.nn.silu(gate.astype(f32)), h[:, -1], ext[:, t:]


def trunk_layer(x, attend, s5_re0, s5_im0, lru_h0, conv0, lp):
    bsz, t = x.shape[0], x.shape[1]
    q, k, v, g_att, q_idx, k_idx, w_idx, u_ssm, g_ssm, u_lru, g_lru = project(x, lp['w_in'])
    q = q.reshape(bsz, t, N_HEADS, HEAD_DIM)
    k = k.reshape(bsz, t, N_HEADS, HEAD_DIM)
    v = v.reshape(bsz, t, N_HEADS, HEAD_DIM)
    q_idx = q_idx.reshape(bsz, t, N_IDX_HEADS, D_IDX)
    o_att = attend(q, k, v, q_idx, k_idx, w_idx).reshape(bsz, t, D_ATT) * jax.nn.silu(g_att)
    o_ssm, s5_re, s5_im = s5_branch(u_ssm, g_ssm, s5_re0, s5_im0, lp)
    o_lru, lru_h, conv = rglru_branch(u_lru, g_lru, conv0, lru_h0, lp)
    mixed = jnp.concatenate([o_att, o_ssm.astype(o_att.dtype), o_lru.astype(o_att.dtype)], axis=-1)
    out = jnp.einsum('bte,ed->btd', mixed, lp['w_out'])
    y = layer_norm(DEEPNORM_ALPHA * x + out, lp['ln_g'], lp['ln_b'])
    return y, (k, v, k_idx, s5_re, s5_im, lru_h, conv)


def setup_inputs(seed: int = 0) -> dict:
    key = jax.random.key(seed)
    ks = jax.random.split(key, 40)
    f32 = jnp.float32
    n_pages = PAST_LEN // PAGE_SIZE
    n_used = DEC_BATCH * n_pages
    n_pool = n_used + max(1, n_used // 4)

    def nrm(k, shape, s):
        return s * jax.random.normal(k, shape, f32)

    page_table = jax.random.permutation(ks[0], n_pool)[:n_used].reshape(DEC_BATCH, n_pages).astype(jnp.int32)
    a_pow = jax.random.uniform(ks[1], (DEPTH, D_LRU), f32, 0.9, 0.999)
    a0 = a_pow ** (1.0 / LRU_C)
    lru_lam = jnp.log(a0) - jnp.log1p(-a0)
    lam_im = jnp.broadcast_to(jnp.pi * jnp.arange(SSM_STATE, dtype=f32), (DEPTH, N_SSM_GROUPS, SSM_STATE))
    return {
        'x_prompt': nrm(ks[2], (BATCH, SEQ, D_MODEL), 1.0),
        'x_sample': nrm(ks[3], (DEC_BATCH, DEC_SEQ, D_MODEL), 1.0),
        'cache_k': nrm(ks[4], (DEPTH, n_pool, PAGE_SIZE, N_HEADS, HEAD_DIM), 1.0),
        'cache_v': nrm(ks[5], (DEPTH, n_pool, PAGE_SIZE, N_HEADS, HEAD_DIM), 1.0),
        'cache_kidx': nrm(ks[6], (DEPTH, n_pool, PAGE_SIZE, D_IDX), 1.0),
        'state_s5_re': nrm(ks[7], (DEPTH, DEC_BATCH, N_SSM_GROUPS, SSM_STATE), 0.1),
        'state_s5_im': nrm(ks[8], (DEPTH, DEC_BATCH, N_SSM_GROUPS, SSM_STATE), 0.1),
        'state_lru_h': nrm(ks[9], (DEPTH, DEC_BATCH, D_LRU), 0.5),
        'state_lru_conv': nrm(ks[10], (DEPTH, DEC_BATCH, CONV_W - 1, D_LRU), 1.0),
        'page_table': page_table,
        'w_in': nrm(ks[11], (DEPTH, D_MODEL, D_IN), D_MODEL ** -0.5),
        'w_out': nrm(ks[12], (DEPTH, D_MIX, D_MODEL), DEEPNORM_BETA * D_MIX ** -0.5),
        'ln_g': 1.0 + nrm(ks[13], (DEPTH, D_MODEL), 0.02),
        'ln_b': nrm(ks[14], (DEPTH, D_MODEL), 0.02),
        'rel_bias': nrm(ks[15], (N_BUCKETS, N_HEADS), 0.1),
        's5_lam_re': -0.5 + nrm(ks[16], (DEPTH, N_SSM_GROUPS, SSM_STATE), 0.01),
        's5_lam_im': lam_im + nrm(ks[17], (DEPTH, N_SSM_GROUPS, SSM_STATE), 0.01),
        's5_log_step': jax.random.uniform(ks[18], (DEPTH, N_SSM_GROUPS), f32, math.log(1e-3), math.log(1e-1)),
        's5_b_re': nrm(ks[19], (DEPTH, N_SSM_GROUPS, SSM_STATE, SSM_CH), (2.0 * SSM_CH) ** -0.5),
        's5_b_im': nrm(ks[20], (DEPTH, N_SSM_GROUPS, SSM_STATE, SSM_CH), (2.0 * SSM_CH) ** -0.5),
        's5_c_re': nrm(ks[21], (DEPTH, N_SSM_GROUPS, SSM_CH, SSM_STATE), (2.0 * SSM_STATE) ** -0.5),
        's5_c_im': nrm(ks[22], (DEPTH, N_SSM_GROUPS, SSM_CH, SSM_STATE), (2.0 * SSM_STATE) ** -0.5),
        's5_d': nrm(ks[23], (DEPTH, D_SSM), 1.0),
        'glu_w': nrm(ks[24], (DEPTH, D_SSM, D_SSM), D_SSM ** -0.5),
        'glu_b': nrm(ks[25], (DEPTH, D_SSM), 0.02),
        'lru_conv_w': nrm(ks[26], (DEPTH, CONV_W, D_LRU), CONV_W ** -0.5),
        'lru_conv_b': nrm(ks[27], (DEPTH, D_LRU), 0.02),
        'lru_wa': nrm(ks[28], (DEPTH, N_LRU_BLOCKS, LRU_BLOCK, LRU_BLOCK), LRU_BLOCK ** -0.5),
        'lru_ba': nrm(ks[29], (DEPTH, D_LRU), 0.02),
        'lru_wx': nrm(ks[30], (DEPTH, N_LRU_BLOCKS, LRU_BLOCK, LRU_BLOCK), LRU_BLOCK ** -0.5),
        'lru_bx': nrm(ks[31], (DEPTH, D_LRU), 0.02),
        'lru_lam': lru_lam,
    }


def reference(x_prompt, x_sample, cache_k, cache_v, cache_kidx, state_s5_re, state_s5_im, state_lru_h,
              state_lru_conv, page_table, w_in, w_out, ln_g, ln_b, rel_bias, s5_lam_re, s5_lam_im,
              s5_log_step, s5_b_re, s5_b_im, s5_c_re, s5_c_im, s5_d, glu_w, glu_b, lru_conv_w, lru_conv_b,
              lru_wa, lru_ba, lru_wx, lru_bx, lru_lam):
    f32 = jnp.float32
    bp = x_prompt.shape[0]
    zero_s5 = jnp.zeros((bp, N_SSM_GROUPS, SSM_STATE), f32)
    zero_h = jnp.zeros((bp, D_LRU), f32)
    zero_conv = jnp.zeros((bp, CONV_W - 1, D_LRU), f32)
    attend_prompt = functools.partial(prompt_attention, rel_bias=rel_bias)
    xp, xs = x_prompt, x_sample
    new_p = [[] for _ in range(7)]
    new_s = [[] for _ in range(7)]
    for l in range(DEPTH):
        lp = {'w_in': w_in[l], 'w_out': w_out[l], 'ln_g': ln_g[l], 'ln_b': ln_b[l],
              'lam_re': s5_lam_re[l], 'lam_im': s5_lam_im[l], 'log_step': s5_log_step[l],
              'b_re': s5_b_re[l], 'b_im': s5_b_im[l], 'c_re': s5_c_re[l], 'c_im': s5_c_im[l],
              'd_skip': s5_d[l], 'glu_w': glu_w[l], 'glu_b': glu_b[l],
              'conv_w': lru_conv_w[l], 'conv_b': lru_conv_b[l], 'wa': lru_wa[l], 'ba': lru_ba[l],
              'wx': lru_wx[l], 'bx': lru_bx[l], 'lam': lru_lam[l]}
        xp, st_p = trunk_layer(xp, attend_prompt, zero_s5, zero_s5, zero_h, zero_conv, lp)
        attend_sample = functools.partial(sample_attention, cache_k=cache_k, cache_v=cache_v,
                                          cache_kidx=cache_kidx, layer=l, page_table=page_table,
                                          rel_bias=rel_bias)
        xs, st_s = trunk_layer(xs, attend_sample, state_s5_re[l], state_s5_im[l], state_lru_h[l],
                               state_lru_conv[l], lp)
        for lst, a in zip(new_p, st_p):
            lst.append(a)
        for lst, a in zip(new_s, st_s):
            lst.append(a)
    k_p, v_p, kidx_p, s5re_p, s5im_p, lruh_p, conv_p = [jnp.stack(a) for a in new_p]
    k_s, v_s, kidx_s, s5re_s, s5im_s, lruh_s, conv_s = [jnp.stack(a) for a in new_s]
    return (xp, xs, k_p, v_p, kidx_p, s5re_p, s5im_p, lruh_p, conv_p,
            k_s, v_s, kidx_s, s5re_s, s5im_s, lruh_s, conv_s)
```

```python
import functools
import math

import numpy as np
import jax
import jax.numpy as jnp
from jax import lax
from jax.experimental import pallas as pl
from jax.experimental.pallas import tpu as pltpu

F32 = jnp.float32
BF16 = jnp.bfloat16
I32 = jnp.int32
HI = lax.Precision.HIGHEST

D_MODEL = 1024
D_ATT = 512
HEAD_DIM = 64
N_HEADS = 8
N_IDX_HEADS = 4
D_IDX = 64
TOPK_MAX = 256
N_BUCKETS = 32
MAX_DISTANCE = 128
D_SSM = 256
SSM_CH = 16
N_SSM_GROUPS = 16
SSM_STATE = 64
S5_P = N_SSM_GROUPS * SSM_STATE
D_LRU = 256
N_LRU_BLOCKS = 4
CONV_W = 4
LRU_C = 8.0
LN_EPS = 1e-5
DEPTH = 4
DEEPNORM_ALPHA = (2.0 * DEPTH) ** 0.25

SUBLANES = 8
LANES = 128
VMEM_LIMIT = 56 * 1024 * 1024

QB = 256
KC = 256
ROW_GROUP = 64
SCAN_CHUNK = 512
MASK_NEG = -1e30
INT_MIN = -2 ** 31
KEY_NEG_INF = -2139095041


def _cparams(n_axes):
    return pltpu.CompilerParams(dimension_semantics=("arbitrary",) * n_axes,
                                vmem_limit_bytes=VMEM_LIMIT)


def _sigmoid(x):
    return 1.0 / (1.0 + jnp.exp(-x))


def _silu(x):
    return x * _sigmoid(x)


def _gelu_tanh(x):
    return 0.5 * x * (1.0 + jnp.tanh(math.sqrt(2.0 / math.pi) * (x + 0.044715 * (x * x * x))))


def _softplus(x):
    return jnp.maximum(x, 0.0) + jnp.log1p(jnp.exp(-jnp.abs(x)))


def _sortable(x):
    bits = pltpu.bitcast(x, I32)
    return bits ^ ((bits >> 31) & 0x7FFFFFFF)


def _mm_kernel(x_ref, w_ref, o_ref):
    o_ref[...] = jnp.dot(x_ref[...], w_ref[...], precision=HI, preferred_element_type=F32)


def _matmul(x, w, name):
    m, k = x.shape
    n = w.shape[1]
    tm = min(512, m)
    return pl.pallas_call(
        _mm_kernel,
        grid=(m // tm,),
        in_specs=[pl.BlockSpec((tm, k), lambda i: (i, 0)),
                  pl.BlockSpec((k, n), lambda i: (0, 0))],
        out_specs=pl.BlockSpec((tm, n), lambda i: (i, 0)),
        out_shape=jax.ShapeDtypeStruct((m, n), F32),
        compiler_params=_cparams(1),
        name=name,
    )(x, w)


def _out_ln_kernel(x_ref, att_ref, gatt_ref, ssm_ref, lru_ref, w_ref, g_ref, b_ref, o_ref):
    att = att_ref[...] * _silu(gatt_ref[...])
    out = jnp.dot(att, w_ref[0:D_ATT, :], precision=HI, preferred_element_type=F32)
    out += jnp.dot(ssm_ref[...], w_ref[D_ATT:D_ATT + D_SSM, :], precision=HI, preferred_element_type=F32)
    out += jnp.dot(lru_ref[...], w_ref[D_ATT + D_SSM:, :], precision=HI, preferred_element_type=F32)
    y = DEEPNORM_ALPHA * x_ref[...] + out
    mu = jnp.mean(y, axis=-1, keepdims=True)
    yc = y - mu
    var = jnp.mean(yc * yc, axis=-1, keepdims=True)
    o_ref[...] = yc * lax.rsqrt(var + LN_EPS) * g_ref[...] + b_ref[...]


def _out_ln(x, att, gatt, ssm, lru, w_out, ln_g, ln_b, name):
    m = x.shape[0]
    tm = min(512, m)
    row = lambda n: pl.BlockSpec((tm, n), lambda i: (i, 0))
    full = lambda a: pl.BlockSpec(a.shape, lambda i: (0,) * a.ndim)
    return pl.pallas_call(
        _out_ln_kernel,
        grid=(m // tm,),
        in_specs=[row(D_MODEL), row(D_ATT), row(D_ATT), row(D_SSM), row(D_LRU),
                  full(w_out), full(ln_g), full(ln_b)],
        out_specs=row(D_MODEL),
        out_shape=jax.ShapeDtypeStruct((m, D_MODEL), F32),
        compiler_params=_cparams(1),
        name=name,
    )(x, att, gatt, ssm, lru, w_out, ln_g, ln_b)


def _rel_bucket_table(n):
    d = np.arange(n)
    exact = N_BUCKETS // 2
    far = exact + (np.log(np.maximum(d, exact).astype(np.float32) / exact)
                   / math.log(MAX_DISTANCE / exact) * (N_BUCKETS - exact)).astype(np.int32)
    return np.where(d < exact, d, np.minimum(far, N_BUCKETS - 1)).astype(np.int32)


def _prompt_attn_kernel(q_ref, kt_ref, v_ref, qi_ref, kit_ref, kw_ref, band_ref, o_ref,
                        key_ref, mb_ref, *, kk, idx_bits):
    i = pl.program_id(1)
    nck = i + 1
    qi = qi_ref[0]
    w = kw_ref[0][:, D_IDX:D_IDX + N_IDX_HEADS] * (N_IDX_HEADS ** -0.5)
    qih = [qi[:, h * D_IDX:(h + 1) * D_IDX] for h in range(N_IDX_HEADS)]
    wh = [w[:, h:h + 1] for h in range(N_IDX_HEADS)]
    qpos = i * QB + lax.broadcasted_iota(I32, (QB, 1), 0)
    lane = lax.broadcasted_iota(I32, (1, KC), 1)

    def score_chunk(c, carry):
        kic = kit_ref[0, c]
        s = jnp.zeros((QB, KC), F32)
        for h in range(N_IDX_HEADS):
            d = jnp.dot(qih[h], kic, precision=HI, preferred_element_type=F32) * (D_IDX ** -0.5)
            s = s + wh[h] * jnp.maximum(d, 0.0)
        kpos = c * KC + lane
        key_ref[c] = jnp.where(kpos <= qpos, _sortable(s), KEY_NEG_INF)
        return carry

    lax.fori_loop(0, nck, score_chunk, 0)

    for rg in range(QB // ROW_GROUP):
        rows = slice(rg * ROW_GROUP, (rg + 1) * ROW_GROUP)

        def count(pred):
            def body(c, acc):
                return acc + pred(c, key_ref[c, rows, :])
            acc = lax.fori_loop(0, nck, body, jnp.zeros((ROW_GROUP, KC), F32))
            return jnp.sum(acc, axis=1, keepdims=True)

        def value_step(it, t):
            cand = t + lax.shift_left(jnp.int32(1), 31 - it)
            cnt = count(lambda c, k: jnp.where(k >= cand, 1.0, 0.0))
            return jnp.where(cnt >= kk, cand, t)

        thr = lax.fori_loop(0, 32, value_step, jnp.full((ROW_GROUP, 1), INT_MIN, I32))
        need = kk - count(lambda c, k: jnp.where(k > thr, 1.0, 0.0))

        def index_step(it, j):
            cand = j + lax.shift_left(jnp.int32(1), idx_bits - 1 - it)
            cnt = count(lambda c, k: jnp.where(k == thr, jnp.where(c * KC + lane < cand, 1.0, 0.0), 0.0))
            return jnp.where(cnt < need, cand, j)

        jlim = lax.fori_loop(0, idx_bits, index_step, jnp.zeros((ROW_GROUP, 1), I32))

        def write_mask(c, carry):
            k = key_ref[c, rows, :]
            tie = jnp.where(c * KC + lane <= jlim, 0.0, MASK_NEG)
            sel = jnp.where(k > thr, 0.0, jnp.where(k == thr, tie, MASK_NEG))
            mb_ref[c, rows, :] = jnp.where(k > KEY_NEG_INF, sel, MASK_NEG)
            return carry

        lax.fori_loop(0, nck, write_mask, 0)

    q = q_ref[0]
    for h in range(N_HEADS):
        hs = slice(h * HEAD_DIM, (h + 1) * HEAD_DIM)
        qh = (q[:, hs] * (HEAD_DIM ** -0.5)).astype(BF16)

        def attend(c, carry):
            m, l, acc = carry
            s = jnp.dot(qh, kt_ref[0, c, hs, :], preferred_element_type=F32)
            s = s + mb_ref[c] + band_ref[jnp.clip(c - i + 2, 0, 2), h]
            m_new = jnp.maximum(m, jnp.max(s, axis=1, keepdims=True))
            alpha = jnp.exp(m - m_new)
            p = jnp.exp(s - m_new)
            l = alpha * l + jnp.sum(p, axis=1, keepdims=True)
            acc = alpha * acc + jnp.dot(p.astype(BF16), v_ref[0, c, :, hs], preferred_element_type=F32)
            return m_new, l, acc

        m0 = jnp.full((QB, 1), MASK_NEG, F32)
        m, l, acc = lax.fori_loop(0, nck, attend,
                                  (m0, jnp.zeros((QB, 1), F32), jnp.zeros((QB, HEAD_DIM), F32)))
        o_ref[0, :, hs] = acc / l


def _prompt_bands(rel_bias):
    tab = _rel_bucket_table(QB + KC)
    qi = np.arange(QB)[:, None]
    sj = np.arange(KC)[None, :]
    diag = tab[np.maximum(qi - sj, 0)]
    prev = tab[qi - sj + KC]
    far = rel_bias[N_BUCKETS - 1]
    t = lambda idx: jnp.transpose(rel_bias[idx] - far, (2, 0, 1))
    return jnp.stack([jnp.zeros((N_HEADS, QB, KC), F32), t(prev), t(diag)])


def _prompt_attention(q, k, v, q_idx, kw, bands):
    b, t = q.shape[0], q.shape[1]
    nc = t // KC
    kk = min(TOPK_MAX, t // 4)
    kt = jnp.transpose(k.reshape(b, nc, KC, D_ATT), (0, 1, 3, 2)).astype(BF16)
    vc = v.reshape(b, nc, KC, D_ATT).astype(BF16)
    kit = jnp.transpose(kw[:, :, :D_IDX].reshape(b, nc, KC, D_IDX), (0, 1, 3, 2))
    blk = lambda n: pl.BlockSpec((1, QB, n), lambda bi, i: (bi, i, 0))
    res = lambda s: pl.BlockSpec((1,) + s, lambda bi, i: (bi, 0, 0, 0))
    kern = functools.partial(_prompt_attn_kernel, kk=kk, idx_bits=max(1, (t - 1).bit_length()))
    return pl.pallas_call(
        kern,
        grid=(b, t // QB),
        in_specs=[blk(D_ATT), res((nc, D_ATT, KC)), res((nc, KC, D_ATT)),
                  blk(N_IDX_HEADS * D_IDX), res((nc, D_IDX, KC)), blk(LANES),
                  pl.BlockSpec(bands.shape, lambda bi, i: (0, 0, 0, 0))],
        out_specs=blk(D_ATT),
        out_shape=jax.ShapeDtypeStruct((b, t, D_ATT), F32),
        scratch_shapes=[pltpu.VMEM((nc, QB, KC), I32), pltpu.VMEM((nc, QB, KC), F32)],
        compiler_params=_cparams(2),
        name="prompt_attention",
    )(q, kt, vc, q_idx, kit, kw, bands)


def _s5_coeffs(lam_re, lam_im, log_step):
    lr = jnp.minimum(lam_re, -1e-4)
    li = lam_im
    dt = jnp.exp(log_step)
    mag = jnp.exp(lr * dt)
    a_re = mag * jnp.cos(li * dt)
    a_im = mag * jnp.sin(li * dt)
    den = lr * lr + li * li
    f_re = ((a_re - 1.0) * lr + a_im * li) / den
    f_im = (a_im * lr - (a_re - 1.0) * li) / den
    return a_re, a_im, f_re, f_im


def _s5_input_drive(u, f_re, f_im, bre_ref, bim_ref):
    w_re = f_re * bre_ref[...] - f_im * bim_ref[...]
    w_im = f_re * bim_ref[...] + f_im * bre_ref[...]
    return (jnp.dot(u, w_re, precision=HI, preferred_element_type=F32),
            jnp.dot(u, w_im, precision=HI, preferred_element_type=F32))


def _s5_readout(h_re, h_im, u, gate, cre_ref, cim_ref, d_ref, gw_ref, gb_ref):
    y = (jnp.dot(h_re, cre_ref[...], precision=HI, preferred_element_type=F32)
         - jnp.dot(h_im, cim_ref[...], precision=HI, preferred_element_type=F32))
    y = y + d_ref[...] * u
    z = _gelu_tanh(y)
    z = z * _sigmoid(jnp.dot(z, gw_ref[...], precision=HI, preferred_element_type=F32) + gb_ref[...])
    return z * _silu(gate)


def _cmul(ar, ai, br, bi):
    return ar * br - ai * bi, ar * bi + ai * br


def _s5_seq_kernel(x_ref, h0re_ref, h0im_ref, lre_ref, lim_ref, ls_ref, bre_ref, bim_ref,
                   cre_ref, cim_ref, d_ref, gw_ref, gb_ref,
                   o_ref, hre_out, him_out, hre_s, him_s, cre_s, cim_s):
    c = pl.program_id(1)
    tc = x_ref.shape[1]

    @pl.when(c == 0)
    def _():
        cre_s[...] = h0re_ref[0]
        cim_s[...] = h0im_ref[0]

    u = x_ref[0, :, 0:D_SSM]
    gate = x_ref[0, :, D_SSM:]
    a_re, a_im, f_re, f_im = _s5_coeffs(lre_ref[...], lim_ref[...], ls_ref[...])
    bu_re, bu_im = _s5_input_drive(u, f_re, f_im, bre_ref, bim_ref)
    hre_s[...] = bu_re
    him_s[...] = bu_im

    full = lambda x: jnp.broadcast_to(x, (SUBLANES, S5_P))
    pows = [(full(a_re), full(a_im))]
    for _ in range(SUBLANES - 1):
        pows.append(_cmul(pows[-1][0], pows[-1][1], pows[0][0], pows[0][1]))
    row = lax.broadcasted_iota(I32, (SUBLANES, S5_P), 0)
    pw_re, pw_im = pows[SUBLANES - 1]
    for r in range(SUBLANES - 2, -1, -1):
        pw_re = jnp.where(row == r, pows[r][0], pw_re)
        pw_im = jnp.where(row == r, pows[r][1], pw_im)

    def tile(j, carry):
        cr, ci = carry
        sl = pl.ds(pl.multiple_of(j * SUBLANES, SUBLANES), SUBLANES)
        xr = hre_s[sl, :]
        xi = him_s[sl, :]
        for s in (1, 2, 4):
            sr = jnp.where(row >= s, pltpu.roll(xr, s, 0), 0.0)
            si = jnp.where(row >= s, pltpu.roll(xi, s, 0), 0.0)
            pr, pi = _cmul(pows[s - 1][0], pows[s - 1][1], sr, si)
            xr = xr + pr
            xi = xi + pi
        pr, pi = _cmul(pw_re, pw_im, cr, ci)
        xr = xr + pr
        xi = xi + pi
        hre_s[sl, :] = xr
        him_s[sl, :] = xi
        return xr[SUBLANES - 1:, :], xi[SUBLANES - 1:, :]

    cr, ci = lax.fori_loop(0, tc // SUBLANES, tile, (cre_s[...], cim_s[...]))
    cre_s[...] = cr
    cim_s[...] = ci
    hre_out[0] = cr
    him_out[0] = ci
    o_ref[0] = _s5_readout(hre_s[...], him_s[...], u, gate, cre_ref, cim_ref, d_ref, gw_ref, gb_ref)


def _s5_step_kernel(x_ref, h0re_ref, h0im_ref, lre_ref, lim_ref, ls_ref, bre_ref, bim_ref,
                    cre_ref, cim_ref, d_ref, gw_ref, gb_ref, o_ref, hre_out, him_out):
    u = x_ref[:, 0:D_SSM]
    gate = x_ref[:, D_SSM:]
    a_re, a_im, f_re, f_im = _s5_coeffs(lre_ref[...], lim_ref[...], ls_ref[...])
    bu_re, bu_im = _s5_input_drive(u, f_re, f_im, bre_ref, bim_ref)
    pr, pi = _cmul(a_re, a_im, h0re_ref[...], h0im_ref[...])
    h_re = bu_re + pr
    h_im = bu_im + pi
    hre_out[...] = h_re
    him_out[...] = h_im
    o_ref[...] = _s5_readout(h_re, h_im, u, gate, cre_ref, cim_ref, d_ref, gw_ref, gb_ref)


def _s5_params(lp):
    eye = jnp.eye(N_SSM_GROUPS, dtype=F32)

    def in_bd(b):
        return (jnp.transpose(b, (0, 2, 1))[:, :, None, :] * eye[:, None, :, None]).reshape(D_SSM, S5_P)

    def out_bd(c):
        return (jnp.transpose(c, (0, 2, 1))[:, :, None, :] * eye[:, None, :, None]).reshape(S5_P, D_SSM)

    ls = jnp.broadcast_to(lp['log_step'][:, None], (N_SSM_GROUPS, SSM_STATE))
    return (lp['lam_re'].reshape(1, S5_P), lp['lam_im'].reshape(1, S5_P), ls.reshape(1, S5_P),
            in_bd(lp['b_re']), in_bd(lp['b_im']), out_bd(lp['c_re']), out_bd(lp['c_im']),
            lp['d_skip'].reshape(1, D_SSM), lp['glu_w'], lp['glu_b'].reshape(1, D_SSM))


def _s5_seq(x, h0_re, h0_im, params):
    b, t = x.shape[0], x.shape[1]
    tc = min(SCAN_CHUNK, t)
    full = lambda a: pl.BlockSpec(a.shape, lambda bi, c: (0,) * a.ndim)
    st = pl.BlockSpec((1, 1, S5_P), lambda bi, c: (bi, 0, 0))
    return pl.pallas_call(
        _s5_seq_kernel,
        grid=(b, t // tc),
        in_specs=[pl.BlockSpec((1, tc, 2 * D_SSM), lambda bi, c: (bi, c, 0)), st, st]
                 + [full(a) for a in params],
        out_specs=[pl.BlockSpec((1, tc, D_SSM), lambda bi, c: (bi, c, 0)), st, st],
        out_shape=[jax.ShapeDtypeStruct((b, t, D_SSM), F32),
                   jax.ShapeDtypeStruct((b, 1, S5_P), F32),
                   jax.ShapeDtypeStruct((b, 1, S5_P), F32)],
        scratch_shapes=[pltpu.VMEM((tc, S5_P), F32), pltpu.VMEM((tc, S5_P), F32),
                        pltpu.VMEM((1, S5_P), F32), pltpu.VMEM((1, S5_P), F32)],
        compiler_params=_cparams(2),
        name="s5_sequence",
    )(x, h0_re, h0_im, *params)


def _s5_step(x, h0_re, h0_im, params):
    b = x.shape[0]
    return pl.pallas_call(
        _s5_step_kernel,
        out_shape=[jax.ShapeDtypeStruct((b, D_SSM), F32),
                   jax.ShapeDtypeStruct((b, S5_P), F32),
                   jax.ShapeDtypeStruct((b, S5_P), F32)],
        compiler_params=pltpu.CompilerParams(vmem_limit_bytes=VMEM_LIMIT),
        name="s5_step",
    )(x, h0_re, h0_im, *params)


def _lru_gates(xc, wa_ref, ba_ref, wx_ref, bx_ref, lam_ref):
    r = _sigmoid(jnp.dot(xc, wa_ref[...], precision=HI, preferred_element_type=F32) + ba_ref[...])
    g = _sigmoid(jnp.dot(xc, wx_ref[...], precision=HI, preferred_element_type=F32) + bx_ref[...])
    log_a = -LRU_C * r * _softplus(-lam_ref[...])
    a = jnp.exp(log_a)
    b = jnp.sqrt(-jnp.tanh(log_a) * (a * a + 1.0)) * (g * xc)
    return a, b


def _lru_seq_kernel(x_ref, conv0_ref, h0_ref, cw_ref, cb_ref, wa_ref, ba_ref, wx_ref, bx_ref, lam_ref,
                    o_ref, h_out, conv_out, ubuf, a_s, b_s, c_s):
    c = pl.program_id(1)
    tc = x_ref.shape[1]
    hist = CONV_W - 1

    @pl.when(c == 0)
    def _():
        ubuf[0:SUBLANES, :] = conv0_ref[0]
        c_s[...] = h0_ref[0]

    u = x_ref[0, :, 0:D_LRU]
    gate = x_ref[0, :, D_LRU:]
    ubuf[SUBLANES:, :] = u
    xc = cb_ref[...] + u * cw_ref[hist:hist + 1, :]
    for j in range(hist):
        xc = xc + ubuf[SUBLANES - hist + j:SUBLANES - hist + j + tc, :] * cw_ref[j:j + 1, :]
    tail = ubuf[tc:tc + SUBLANES, :]
    ubuf[0:SUBLANES, :] = tail
    conv_out[0] = tail

    a, b = _lru_gates(xc, wa_ref, ba_ref, wx_ref, bx_ref, lam_ref)
    a_s[...] = a
    b_s[...] = b
    row = lax.broadcasted_iota(I32, (SUBLANES, D_LRU), 0)

    def tile(j, carry):
        sl = pl.ds(pl.multiple_of(j * SUBLANES, SUBLANES), SUBLANES)
        av = a_s[sl, :]
        bv = b_s[sl, :]
        for s in (1, 2, 4):
            a_sh = jnp.where(row >= s, pltpu.roll(av, s, 0), 1.0)
            b_sh = jnp.where(row >= s, pltpu.roll(bv, s, 0), 0.0)
            bv = av * b_sh + bv
            av = av * a_sh
        h = bv + av * carry
        b_s[sl, :] = h
        return h[SUBLANES - 1:, :]

    carry = lax.fori_loop(0, tc // SUBLANES, tile, c_s[...])
    c_s[...] = carry
    h_out[0] = carry
    o_ref[0] = b_s[...] * _silu(gate)


def _lru_step_kernel(x_ref, c0_ref, c1_ref, c2_ref, h0_ref, cw_ref, cb_ref, wa_ref, ba_ref, wx_ref,
                     bx_ref, lam_ref, o_ref, h_out):
    u = x_ref[:, 0:D_LRU]
    gate = x_ref[:, D_LRU:]
    xc = (cb_ref[...] + c0_ref[...] * cw_ref[0:1, :] + c1_ref[...] * cw_ref[1:2, :]
          + c2_ref[...] * cw_ref[2:3, :] + u * cw_ref[3:4, :])
    a, b = _lru_gates(xc, wa_ref, ba_ref, wx_ref, bx_ref, lam_ref)
    h = a * h0_ref[...] + b
    h_out[...] = h
    o_ref[...] = h * _silu(gate)


def _lru_params(lp):
    eye = jnp.eye(N_LRU_BLOCKS, dtype=F32)
    bd = lambda w: (w[:, :, None, :] * eye[:, None, :, None]).reshape(D_LRU, D_LRU)
    cw = jnp.concatenate([lp['conv_w'], jnp.zeros((SUBLANES - CONV_W, D_LRU), F32)], axis=0)
    return (cw, lp['conv_b'].reshape(1, D_LRU), bd(lp['wa']), lp['ba'].reshape(1, D_LRU),
            bd(lp['wx']), lp['bx'].reshape(1, D_LRU), lp['lam'].reshape(1, D_LRU))


def _lru_seq(x, conv0, h0, params):
    b, t = x.shape[0], x.shape[1]
    tc = min(SCAN_CHUNK, t)
    full = lambda a: pl.BlockSpec(a.shape, lambda bi, c: (0,) * a.ndim)
    return pl.pallas_call(
        _lru_seq_kernel,
        grid=(b, t // tc),
        in_specs=[pl.BlockSpec((1, tc, 2 * D_LRU), lambda bi, c: (bi, c, 0)),
                  pl.BlockSpec((1, SUBLANES, D_LRU), lambda bi, c: (bi, 0, 0)),
                  pl.BlockSpec((1, 1, D_LRU), lambda bi, c: (bi, 0, 0))]
                 + [full(a) for a in params],
        out_specs=[pl.BlockSpec((1, tc, D_LRU), lambda bi, c: (bi, c, 0)),
                   pl.BlockSpec((1, 1, D_LRU), lambda bi, c: (bi, 0, 0)),
                   pl.BlockSpec((1, SUBLANES, D_LRU), lambda bi, c: (bi, 0, 0))],
        out_shape=[jax.ShapeDtypeStruct((b, t, D_LRU), F32),
                   jax.ShapeDtypeStruct((b, 1, D_LRU), F32),
                   jax.ShapeDtypeStruct((b, SUBLANES, D_LRU), F32)],
        scratch_shapes=[pltpu.VMEM((tc + SUBLANES, D_LRU), F32), pltpu.VMEM((tc, D_LRU), F32),
                        pltpu.VMEM((tc, D_LRU), F32), pltpu.VMEM((1, D_LRU), F32)],
        compiler_params=_cparams(2),
        name="lru_sequence",
    )(x, conv0, h0, *params)


def _lru_step(x, conv0, h0, params):
    b = x.shape[0]
    return pl.pallas_call(
        _lru_step_kernel,
        out_shape=[jax.ShapeDtypeStruct((b, D_LRU), F32), jax.ShapeDtypeStruct((b, D_LRU), F32)],
        compiler_params=pltpu.CompilerParams(vmem_limit_bytes=VMEM_LIMIT),
        name="lru_step",
    )(x, conv0[:, 0], conv0[:, 1], conv0[:, 2], h0, *params)


PAGES_PER_STEP = 8


def _dec_score_kernel(pt_ref, qi_ref, w_ref, *refs):
    pages, o_ref = refs[:-1], refs[-1]
    qi = qi_ref[0]
    w = w_ref[0] * (N_IDX_HEADS ** -0.5)
    for g, page in enumerate(pages):
        d = lax.dot_general(qi, page[0, 0], (((1,), (1,)), ((), ())),
                            precision=HI, preferred_element_type=F32) * (D_IDX ** -0.5)
        o_ref[0, 0, g:g + 1, :] = jnp.sum(w * jnp.maximum(d, 0.0), axis=0, keepdims=True)


def _dec_scores(q_idx, w_idx, cache_kidx, page_table, layer):
    b = q_idx.shape[0]
    n_pages = page_table.shape[1]
    page = cache_kidx.shape[2]
    g = PAGES_PER_STEP
    page_spec = lambda j: pl.BlockSpec(
        (1, 1, page, D_IDX), lambda bi, p, pt: (layer, pt[bi * n_pages + p * g + j], 0, 0))
    out = pl.pallas_call(
        _dec_score_kernel,
        grid_spec=pltpu.PrefetchScalarGridSpec(
            num_scalar_prefetch=1,
            grid=(b, n_pages // g),
            in_specs=[pl.BlockSpec((1, N_IDX_HEADS, D_IDX), lambda bi, p, pt: (bi, 0, 0)),
                      pl.BlockSpec((1, N_IDX_HEADS, 1), lambda bi, p, pt: (bi, 0, 0))]
                     + [page_spec(j) for j in range(g)],
            out_specs=pl.BlockSpec((1, 1, g, page), lambda bi, p, pt: (bi, p, 0, 0)),
        ),
        out_shape=jax.ShapeDtypeStruct((b, n_pages // g, g, page), F32),
        compiler_params=_cparams(2),
        name="decode_scores",
    )(page_table.reshape(-1), q_idx.reshape(b, N_IDX_HEADS, D_IDX),
      w_idx.reshape(b, N_IDX_HEADS, 1), *([cache_kidx] * g))
    return out.reshape(b, n_pages * page)


def _dec_select_kernel(s_ref, qi_ref, kn_ref, w_ref, idx_ref, cnt_ref, rank_ref, *, kk, idx_bits):
    b, past = s_ref.shape
    n = past + LANES
    qi = qi_ref[...]
    kn = kn_ref[...]
    w = w_ref[...] * (N_IDX_HEADS ** -0.5)
    s_new = jnp.zeros((b, 1), F32)
    for h in range(N_IDX_HEADS):
        d = jnp.sum(qi[:, h * D_IDX:(h + 1) * D_IDX] * kn, axis=1, keepdims=True) * (D_IDX ** -0.5)
        s_new = s_new + w[:, h:h + 1] * jnp.maximum(d, 0.0)
    lane = lax.broadcasted_iota(I32, (1, LANES), 1)
    tail = jnp.where(lane == 0, _sortable(jnp.broadcast_to(s_new, (b, LANES))), KEY_NEG_INF)
    keys = jnp.concatenate([_sortable(s_ref[...]), tail], axis=1)
    pos = lax.broadcasted_iota(I32, (1, n), 1)

    def count(m):
        return jnp.sum(m, axis=1, keepdims=True)

    def value_step(it, t):
        cand = t + lax.shift_left(jnp.int32(1), 31 - it)
        return jnp.where(count(jnp.where(keys >= cand, 1.0, 0.0)) >= kk, cand, t)

    thr = lax.fori_loop(0, 32, value_step, jnp.full((b, 1), INT_MIN, I32))
    need = kk - count(jnp.where(keys > thr, 1.0, 0.0))

    def index_step(it, j):
        cand = j + lax.shift_left(jnp.int32(1), idx_bits - 1 - it)
        c = count(jnp.where(keys == thr, jnp.where(pos < cand, 1.0, 0.0), 0.0))
        return jnp.where(c < need, cand, j)

    jlim = lax.fori_loop(0, idx_bits, index_step, jnp.zeros((b, 1), I32))
    sel = jnp.where(keys > thr, 1.0, jnp.where(keys == thr, jnp.where(pos <= jlim, 1.0, 0.0), 0.0))
    sel = jnp.where(keys > KEY_NEG_INF, sel, 0.0)
    cnt_ref[...] = count(sel)

    tri = jnp.where(lax.broadcasted_iota(I32, (LANES, LANES), 0) < lax.broadcasted_iota(I32, (LANES, LANES), 1),
                    1.0, 0.0).astype(BF16)
    off = jnp.zeros((b, 1), F32)
    ranks = []
    for t in range(n // LANES):
        blk = sel[:, t * LANES:(t + 1) * LANES]
        ranks.append(jnp.dot(blk.astype(BF16), tri, preferred_element_type=F32) + off)
        off = off + count(blk)
    rank_ref[...] = jnp.where(sel > 0.0, jnp.concatenate(ranks, axis=1), -1.0)

    slot = lax.broadcasted_iota(I32, (kk, 1), 0).astype(F32)
    posf = pos.astype(F32)

    def compact_row(r, carry):
        hit = jnp.where(rank_ref[pl.ds(r, 1), :] == slot, posf, 0.0)
        idx_ref[r] = jnp.sum(hit, axis=1, keepdims=True).astype(I32)
        return carry

    lax.fori_loop(0, b, compact_row, 0)


def _dec_select(scores, q_idx, k_idx_new, w_idx, kk):
    b, past = scores.shape
    kern = functools.partial(_dec_select_kernel, kk=kk, idx_bits=max(1, (past + LANES - 1).bit_length()))
    idx, cnt = pl.pallas_call(
        kern,
        out_shape=[jax.ShapeDtypeStruct((b, kk, 1), I32), jax.ShapeDtypeStruct((b, 1), F32)],
        scratch_shapes=[pltpu.VMEM((b, past + LANES), F32)],
        compiler_params=pltpu.CompilerParams(vmem_limit_bytes=VMEM_LIMIT),
        name="decode_select",
    )(scores, q_idx, k_idx_new, w_idx)
    return idx.reshape(b, kk), cnt.astype(I32).reshape(b)


def _dec_attn_kernel(pt_ref, idx_ref, cnt_ref, q_ref, kn_ref, vn_ref, pos_ref, rb_ref, ck_ref, cv_ref,
                     o_ref, kbuf, vbuf, sem, *, layer, kk, n_pages, page, past):
    bi = pl.program_id(0)

    def row_copies(j):
        idx = idx_ref[bi * kk + j]
        pg = pt_ref[bi * n_pages + jnp.minimum(idx // page, n_pages - 1)]
        off = idx % page
        return (pltpu.make_async_copy(ck_ref.at[layer, pg, off], kbuf.at[j], sem.at[0]),
                pltpu.make_async_copy(cv_ref.at[layer, pg, off], vbuf.at[j], sem.at[1]))

    def start(j, carry):
        ck, cv = row_copies(j)
        ck.start()
        cv.start()
        return carry

    def wait(j, carry):
        ck, cv = row_copies(j)
        ck.wait()
        cv.wait()
        return carry

    lax.fori_loop(0, kk, start, 0)
    lax.fori_loop(0, kk, wait, 0)

    pos = pos_ref[0]
    is_new = pos >= past
    k = jnp.where(is_new, kn_ref[...], kbuf[...])
    v = jnp.where(is_new, vn_ref[...], vbuf[...])
    q = q_ref[...]
    logits = jnp.sum(k * q, axis=2, keepdims=True) * (HEAD_DIM ** -0.5)

    dist = jnp.maximum(past - pos, 0)
    exact = N_BUCKETS // 2
    far = exact + (jnp.log(jnp.maximum(dist, exact).astype(F32) / exact)
                   / math.log(MAX_DISTANCE / exact) * (N_BUCKETS - exact)).astype(I32)
    bucket = jnp.where(dist < exact, dist, jnp.minimum(far, N_BUCKETS - 1))
    bias = jnp.zeros((kk, N_HEADS, 1), F32)
    for bkt in range(N_BUCKETS):
        bias = jnp.where(bucket == bkt, rb_ref[bkt:bkt + 1], bias)
    slot = lax.broadcasted_iota(I32, (kk, 1, 1), 0)
    logits = jnp.where(slot < cnt_ref[bi], logits + bias, -jnp.inf)
    m = jnp.max(logits, axis=0, keepdims=True)
    p = jnp.exp(logits - m)
    p = p / jnp.sum(p, axis=0, keepdims=True)
    o_ref[...] = jnp.sum(p * v, axis=0, keepdims=True)


def _dec_attention(q, k_new, v_new, sel_idx, sel_cnt, cache_k, cache_v, page_table, rel_bias, layer):
    b = q.shape[0]
    kk = sel_idx.shape[1]
    n_pages = page_table.shape[1]
    page = cache_k.shape[2]
    kern = functools.partial(_dec_attn_kernel, layer=layer, kk=kk, n_pages=n_pages, page=page,
                             past=n_pages * page)
    row = lambda: pl.BlockSpec((1, N_HEADS, HEAD_DIM), lambda bi, pt, ix, ct: (bi, 0, 0))
    return pl.pallas_call(
        kern,
        grid_spec=pltpu.PrefetchScalarGridSpec(
            num_scalar_prefetch=3,
            grid=(b,),
            in_specs=[row(), row(), row(),
                      pl.BlockSpec((1, kk, 1, 1), lambda bi, pt, ix, ct: (bi, 0, 0, 0)),
                      pl.BlockSpec((N_BUCKETS, N_HEADS, 1), lambda bi, pt, ix, ct: (0, 0, 0)),
                      pl.BlockSpec(memory_space=pl.ANY),
                      pl.BlockSpec(memory_space=pl.ANY)],
            out_specs=row(),
            scratch_shapes=[pltpu.VMEM((kk, N_HEADS, HEAD_DIM), F32),
                            pltpu.VMEM((kk, N_HEADS, HEAD_DIM), F32),
                            pltpu.SemaphoreType.DMA((2,))],
        ),
        out_shape=jax.ShapeDtypeStruct((b, N_HEADS, HEAD_DIM), F32),
        compiler_params=_cparams(1),
        name="decode_attention",
    )(page_table.reshape(-1), sel_idx.reshape(-1), sel_cnt,
      q.reshape(b, N_HEADS, HEAD_DIM), k_new.reshape(b, N_HEADS, HEAD_DIM),
      v_new.reshape(b, N_HEADS, HEAD_DIM), sel_idx.reshape(b, kk, 1, 1),
      rel_bias.reshape(N_BUCKETS, N_HEADS, 1), cache_k, cache_v)


def _split_w_in(w_in):
    widths = (D_ATT, D_ATT, D_ATT, D_ATT, N_IDX_HEADS * D_IDX, D_IDX, N_IDX_HEADS,
              D_SSM, D_SSM, D_LRU, D_LRU)
    cuts = np.concatenate([[0], np.cumsum(widths)])
    seg = lambda a, b_: w_in[:, cuts[a]:cuts[b_]]
    kw = jnp.concatenate([seg(5, 7), jnp.zeros((D_MODEL, LANES - D_IDX - N_IDX_HEADS), F32)], axis=1)
    return dict(q=seg(0, 1), k=seg(1, 2), v=seg(2, 3), g=seg(3, 4), qi=seg(4, 5), kw=kw,
                ssm=seg(7, 9), lru=seg(9, 11))


def _project(x2d, ws, tag):
    return {n: _matmul(x2d, w, f"in_proj_{n}_{tag}") for n, w in ws.items()}


def kernel(x_prompt, x_sample, cache_k, cache_v, cache_kidx, state_s5_re, state_s5_im, state_lru_h,
           state_lru_conv, page_table, w_in, w_out, ln_g, ln_b, rel_bias, s5_lam_re, s5_lam_im,
           s5_log_step, s5_b_re, s5_b_im, s5_c_re, s5_c_im, s5_d, glu_w, glu_b, lru_conv_w, lru_conv_b,
           lru_wa, lru_ba, lru_wx, lru_bx, lru_lam):
    depth = w_in.shape[0]
    bp, t = x_prompt.shape[0], x_prompt.shape[1]
    bs = x_sample.shape[0]
    n_pages = page_table.shape[1]
    page = cache_k.shape[2]
    past = n_pages * page
    kk_s = min(TOPK_MAX, (past + 1) // 4)
    bands = _prompt_bands(rel_bias)
    hist = CONV_W - 1

    xp = x_prompt.reshape(bp * t, D_MODEL)
    xs = x_sample.reshape(bs, D_MODEL)
    zero_state = jnp.zeros((bp, 1, S5_P), F32)
    zero_h = jnp.zeros((bp, 1, D_LRU), F32)
    zero_conv = jnp.zeros((bp, SUBLANES, D_LRU), F32)
    new_p = [[] for _ in range(7)]
    new_s = [[] for _ in range(7)]

    for l in range(depth):
        ws = _split_w_in(w_in[l])
        s5p = _s5_params({'lam_re': s5_lam_re[l], 'lam_im': s5_lam_im[l], 'log_step': s5_log_step[l],
                          'b_re': s5_b_re[l], 'b_im': s5_b_im[l], 'c_re': s5_c_re[l], 'c_im': s5_c_im[l],
                          'd_skip': s5_d[l], 'glu_w': glu_w[l], 'glu_b': glu_b[l]})
        lrup = _lru_params({'conv_w': lru_conv_w[l], 'conv_b': lru_conv_b[l], 'wa': lru_wa[l],
                            'ba': lru_ba[l], 'wx': lru_wx[l], 'bx': lru_bx[l], 'lam': lru_lam[l]})
        lng = ln_g[l].reshape(1, D_MODEL)
        lnb = ln_b[l].reshape(1, D_MODEL)

        hp = _project(xp, ws, "prompt")
        r3 = lambda a: a.reshape(bp, t, a.shape[-1])
        o_att = _prompt_attention(r3(hp['q']), r3(hp['k']), r3(hp['v']), r3(hp['qi']), r3(hp['kw']), bands)
        o_ssm, s5re, s5im = _s5_seq(r3(hp['ssm']), zero_state, zero_state, s5p)
        o_lru, lruh, conv = _lru_seq(r3(hp['lru']), zero_conv, zero_h, lrup)
        flat = lambda a: a.reshape(bp * t, a.shape[-1])
        xp_new = _out_ln(xp, flat(o_att), hp['g'], flat(o_ssm), flat(o_lru), w_out[l], lng, lnb,
                         "out_ln_prompt")
        for lst, a in zip(new_p, (hp['k'].reshape(bp, t, N_HEADS, HEAD_DIM),
                                  hp['v'].reshape(bp, t, N_HEADS, HEAD_DIM),
                                  r3(hp['kw'])[:, :, :D_IDX],
                                  s5re.reshape(bp, N_SSM_GROUPS, SSM_STATE),
                                  s5im.reshape(bp, N_SSM_GROUPS, SSM_STATE),
                                  lruh.reshape(bp, D_LRU), conv[:, SUBLANES - hist:, :])):
            lst.append(a)
        xp = xp_new

        hs = _project(xs, ws, "sample")
        kidx_s = hs['kw'][:, :D_IDX]
        widx_s = hs['kw'][:, D_IDX:D_IDX + N_IDX_HEADS]
        scores = _dec_scores(hs['qi'], widx_s, cache_kidx, page_table, l)
        sel_idx, sel_cnt = _dec_select(scores, hs['qi'], kidx_s, widx_s, kk_s)
        o_att_s = _dec_attention(hs['q'], hs['k'], hs['v'], sel_idx, sel_cnt, cache_k, cache_v,
                                 page_table, rel_bias, l)
        o_ssm_s, s5re_s, s5im_s = _s5_step(hs['ssm'], state_s5_re[l].reshape(bs, S5_P),
                                           state_s5_im[l].reshape(bs, S5_P), s5p)
        conv0 = state_lru_conv[l]
        o_lru_s, lruh_s = _lru_step(hs['lru'], conv0, state_lru_h[l], lrup)
        conv_s = jnp.concatenate([conv0[:, 1:], hs['lru'][:, None, :D_LRU]], axis=1)
        xs_new = _out_ln(xs, o_att_s.reshape(bs, D_ATT), hs['g'], o_ssm_s, o_lru_s, w_out[l], lng, lnb,
                         "out_ln_sample")
        for lst, a in zip(new_s, (hs['k'].reshape(bs, 1, N_HEADS, HEAD_DIM),
                                  hs['v'].reshape(bs, 1, N_HEADS, HEAD_DIM),
                                  kidx_s.reshape(bs, 1, D_IDX),
                                  s5re_s.reshape(bs, N_SSM_GROUPS, SSM_STATE),
                                  s5im_s.reshape(bs, N_SSM_GROUPS, SSM_STATE),
                                  lruh_s, conv_s)):
            lst.append(a)
        xs = xs_new

    outs_p = [jnp.stack(a) for a in new_p]
    outs_s = [jnp.stack(a) for a in new_s]
    return (xp.reshape(bp, t, D_MODEL), xs.reshape(bs, 1, D_MODEL), *outs_p, *outs_s)
```

```python
import functools
import math

import numpy as np
import jax
import jax.numpy as jnp
from jax import lax
from jax.experimental import pallas as pl
from jax.experimental.pallas import tpu as pltpu

F32 = jnp.float32
BF16 = jnp.bfloat16
I32 = jnp.int32
I16 = jnp.int16
I16_MIN = -2 ** 15
I16_MAX = 2 ** 15 - 1
HI = lax.Precision.HIGHEST

D_MODEL = 1024
D_ATT = 512
HEAD_DIM = 64
N_HEADS = 8
N_IDX_HEADS = 4
D_IDX = 64
TOPK_MAX = 256
N_BUCKETS = 32
MAX_DISTANCE = 128
D_SSM = 256
SSM_CH = 16
N_SSM_GROUPS = 16
SSM_STATE = 64
S5_P = N_SSM_GROUPS * SSM_STATE
D_LRU = 256
N_LRU_BLOCKS = 4
CONV_W = 4
LRU_C = 8.0
LN_EPS = 1e-5
DEPTH = 4
DEEPNORM_ALPHA = (2.0 * DEPTH) ** 0.25
LOG2E = math.log2(math.e)

SUBLANES = 8
LANES = 128
VMEM_LIMIT = 56 * 1024 * 1024

QB = 256
KC = 256
KS = 256
N_NEAR = QB // KS + 1
HEAD_GROUP = 4
SCAN_CHUNK = 512
SCORE_PAGES_PER_STEP = 32
ATTN_PAGES_PER_STEP = 16
MASK_NEG = -1e30
INT_MIN = -2 ** 31
KEY_NEG_INF = -2139095041
NO_TIE_LIMIT = 2 ** 30


def _cparams(n_axes, flags=None):
    return pltpu.CompilerParams(dimension_semantics=("arbitrary",) * n_axes,
                                vmem_limit_bytes=VMEM_LIMIT, flags=flags)


def _bdot(a, b):
    return jnp.dot(a.astype(BF16), b.astype(BF16), preferred_element_type=F32)


def _sigmoid(x):
    return 1.0 / (1.0 + jnp.exp(-x))


def _silu(x):
    return x * _sigmoid(x)


def _gelu_tanh(x):
    return 0.5 * x * (1.0 + jnp.tanh(math.sqrt(2.0 / math.pi) * (x + 0.044715 * (x * x * x))))


def _softplus(x):
    return jnp.maximum(x, 0.0) + jnp.log1p(jnp.exp(-jnp.abs(x)))


def _sortable(x):
    bits = pltpu.bitcast(x, I32)
    return bits ^ ((bits >> 31) & 0x7FFFFFFF)


def _rel_bucket_table(n):
    d = np.arange(n)
    exact = N_BUCKETS // 2
    far = exact + (np.log(np.maximum(d, exact).astype(np.float32) / exact)
                   / math.log(MAX_DISTANCE / exact) * (N_BUCKETS - exact)).astype(np.int32)
    return np.where(d < exact, d, np.minimum(far, N_BUCKETS - 1)).astype(np.int32)


IN_WIDTHS = dict(q=D_ATT, k=D_ATT, v=D_ATT, g=D_ATT, qi=N_IDX_HEADS * D_IDX, kw=LANES,
                 ssm=2 * D_SSM, lru=2 * D_LRU)


def _in_proj_kernel(x_ref, w_ref, *o_refs):
    x = x_ref[...].astype(BF16)
    off = 0
    for o_ref in o_refs:
        n = o_ref.shape[1]
        o_ref[...] = jnp.dot(x, w_ref[:, off:off + n], preferred_element_type=F32)
        off += n


def _pack_w_in(w_in):
    widths = (D_ATT, D_ATT, D_ATT, D_ATT, N_IDX_HEADS * D_IDX, D_IDX, N_IDX_HEADS,
              D_SSM, D_SSM, D_LRU, D_LRU)
    cuts = np.concatenate([[0], np.cumsum(widths)])
    pad = jnp.zeros((D_MODEL, LANES - D_IDX - N_IDX_HEADS), w_in.dtype)
    return jnp.concatenate([w_in[:, :cuts[7]], pad, w_in[:, cuts[7]:]], axis=1).astype(BF16)


def _in_proj(x2d, w_packed, name):
    m = x2d.shape[0]
    tm = min(512, m)
    n_all = w_packed.shape[1]
    outs = pl.pallas_call(
        _in_proj_kernel,
        grid=(m // tm,),
        in_specs=[pl.BlockSpec((tm, D_MODEL), lambda i: (i, 0)),
                  pl.BlockSpec((D_MODEL, n_all), lambda i: (0, 0))],
        out_specs=[pl.BlockSpec((tm, n), lambda i: (i, 0)) for n in IN_WIDTHS.values()],
        out_shape=[jax.ShapeDtypeStruct((m, n), F32) for n in IN_WIDTHS.values()],
        compiler_params=_cparams(1),
        name=name,
    )(x2d, w_packed)
    return dict(zip(IN_WIDTHS.keys(), outs))


IN_T_GROUPS = ('q', 'k', 'v', 'qi', 'kw')
IN_N_GROUPS = ('k', 'g', 'kw', 'ssm', 'lru')
IN_N_DTYPES = dict(k=BF16, g=F32, kw=BF16, ssm=F32, lru=F32)


def _in_proj_prompt_kernel(x_ref, wt_ref, wn_ref, *o_refs):
    nt_refs = o_refs[:len(IN_T_GROUPS)]
    vtc_ref = o_refs[len(IN_T_GROUPS)]
    nn_refs = o_refs[len(IN_T_GROUPS) + 1:]
    x = x_ref[0].astype(BF16)
    off = 0
    for name, o_ref in zip(IN_T_GROUPS, nt_refs):
        n = o_ref.shape[1]
        res = lax.dot_general(wt_ref[off:off + n, :], x, (((1,), (1,)), ((), ())),
                              preferred_element_type=F32)
        o_ref[0] = res
        if name == 'v':
            for j in range(vtc_ref.shape[1]):
                vtc_ref[0, j] = res[:, j * KS:(j + 1) * KS].astype(BF16)
        off += n
    off = 0
    for o_ref in nn_refs:
        n = o_ref.shape[2]
        o_ref[0] = jnp.dot(x, wn_ref[:, off:off + n], preferred_element_type=F32).astype(o_ref.dtype)
        off += n


def _in_proj_prompt(x, w_packed):
    b, t = x.shape[0], x.shape[1]
    tm = min(512, t)
    cols = {}
    off = 0
    for name, n in IN_WIDTHS.items():
        cols[name] = (off, off + n)
        off += n
    pick = lambda names: jnp.concatenate([w_packed[:, cols[n][0]:cols[n][1]] for n in names], axis=1)
    wt = jnp.transpose(pick(IN_T_GROUPS))
    wn = pick(IN_N_GROUPS)
    t_spec = lambda n: pl.BlockSpec((1, n, tm), lambda bi, j: (bi, 0, j))
    n_spec = lambda n: pl.BlockSpec((1, tm, n), lambda bi, j: (bi, j, 0))
    outs = pl.pallas_call(
        _in_proj_prompt_kernel,
        grid=(b, t // tm),
        in_specs=[n_spec(D_MODEL),
                  pl.BlockSpec(wt.shape, lambda bi, j: (0, 0)),
                  pl.BlockSpec(wn.shape, lambda bi, j: (0, 0))],
        out_specs=[t_spec(IN_WIDTHS[n]) for n in IN_T_GROUPS]
                  + [pl.BlockSpec((1, tm // KS, D_ATT, KS), lambda bi, j: (bi, j, 0, 0))]
                  + [n_spec(IN_WIDTHS[n]) for n in IN_N_GROUPS],
        out_shape=[jax.ShapeDtypeStruct((b, IN_WIDTHS[n], t), F32) for n in IN_T_GROUPS]
                  + [jax.ShapeDtypeStruct((b, t // KS, D_ATT, KS), BF16)]
                  + [jax.ShapeDtypeStruct((b, t, IN_WIDTHS[n]), IN_N_DTYPES[n]) for n in IN_N_GROUPS],
        compiler_params=_cparams(2),
        name="in_proj_prompt",
    )(x, wt, wn)
    res = {n + 'T': o for n, o in zip(IN_T_GROUPS, outs)}
    res['vtc'] = outs[len(IN_T_GROUPS)]
    res.update(zip(IN_N_GROUPS, outs[len(IN_T_GROUPS) + 1:]))
    return res


def _out_ln_kernel(x_ref, att_ref, gatt_ref, ssm_ref, lru_ref, w_ref, g_ref, b_ref, o_ref):
    att = att_ref[...] * _silu(gatt_ref[...])
    out = _bdot(att, w_ref[0:D_ATT, :])
    out += _bdot(ssm_ref[...], w_ref[D_ATT:D_ATT + D_SSM, :])
    out += _bdot(lru_ref[...], w_ref[D_ATT + D_SSM:, :])
    y = DEEPNORM_ALPHA * x_ref[...] + out
    mu = jnp.mean(y, axis=-1, keepdims=True)
    yc = y - mu
    var = jnp.mean(yc * yc, axis=-1, keepdims=True)
    o_ref[...] = yc * lax.rsqrt(var + LN_EPS) * g_ref[...] + b_ref[...]


def _out_ln(x, att, gatt, ssm, lru, w_out, ln_g, ln_b, name):
    m = x.shape[0]
    tm = min(512, m)
    row = lambda n: pl.BlockSpec((tm, n), lambda i: (i, 0))
    full = lambda a: pl.BlockSpec(a.shape, lambda i: (0,) * a.ndim)
    return pl.pallas_call(
        _out_ln_kernel,
        grid=(m // tm,),
        in_specs=[row(D_MODEL), row(D_ATT), row(D_ATT), row(D_SSM), row(D_LRU),
                  full(w_out), full(ln_g), full(ln_b)],
        out_specs=row(D_MODEL),
        out_shape=jax.ShapeDtypeStruct((m, D_MODEL), F32),
        compiler_params=_cparams(1),
        name=name,
    )(x, att, gatt, ssm, lru, w_out, ln_g, ln_b)


def _prompt_attn_kernel(qt_ref, k_ref, vt_ref, qit_ref, ki_ref, wt_ref, band_ref, ot_ref,
                        key_ref, half_ref, mb_ref, m_ref, l_ref, qbd_ref, *, kk, idx_bits):
    i = pl.program_id(1)
    nck = i + 1
    qit = qit_ref[0].astype(BF16)
    zero_rows = jnp.zeros((LANES - D_IDX, QB), BF16)
    qih = [jnp.concatenate([qit[h * D_IDX:(h + 1) * D_IDX, :], zero_rows], axis=0)
           for h in range(N_IDX_HEADS)]
    w = wt_ref[0] * (N_IDX_HEADS ** -0.5 * D_IDX ** -0.5)
    qpos = i * QB + lax.broadcasted_iota(I32, (1, QB), 1)
    row = lax.broadcasted_iota(I32, (KC, 1), 0)

    def score_chunk(c, carry):
        kic = ki_ref[0, c]
        s = jnp.zeros((KC, QB), F32)
        for h in range(N_IDX_HEADS):
            d = jnp.dot(kic, qih[h], preferred_element_type=F32)
            s = s + w[h:h + 1, :] * jnp.maximum(d, 0.0)
        key = jnp.where(c * KC + row <= qpos, _sortable(s), KEY_NEG_INF)
        key_ref[c] = key
        half_ref[c] = (key >> 16).astype(I16)
        return carry

    lax.fori_loop(0, nck, score_chunk, 0)

    def count(pred):
        def body(c, acc):
            hit = pred(c, key_ref[c])
            return acc + jnp.sum(hit.reshape(KC // SUBLANES, SUBLANES, QB), axis=0)
        acc = lax.fori_loop(0, nck, body, jnp.zeros((SUBLANES, QB), F32))
        return jnp.sum(acc, axis=0, keepdims=True)

    half_tile = 2 * SUBLANES

    def count_half_ge(cand):
        cand16 = cand.astype(I16)

        def body(c, acc):
            hit = jnp.where(half_ref[c] >= cand16, jnp.int16(1), jnp.int16(0))
            hit = hit.reshape(KC // half_tile, half_tile, QB)
            for r in range(KC // half_tile):
                acc = acc + hit[r]
            return acc

        acc = lax.fori_loop(0, nck, body, jnp.zeros((half_tile, QB), I16))
        return jnp.sum(acc.astype(I32), axis=0, keepdims=True)

    def kth_largest_half(rank):
        def step(it, t):
            cand = t + lax.shift_left(jnp.int32(1), 15 - it)
            return jnp.where(count_half_ge(cand) >= rank, cand, t)
        return lax.fori_loop(0, 16, step, jnp.full((1, QB), I16_MIN, I32))

    t_hi = kth_largest_half(kk)
    above = jnp.where(t_hi < I16_MAX, count_half_ge(jnp.minimum(t_hi + 1, I16_MAX)), 0)

    def low_halves(c, carry):
        key = key_ref[c]
        low = (key & 0xFFFF) + I16_MIN
        half_ref[c] = jnp.where((key >> 16) == t_hi, low, I16_MIN).astype(I16)
        return carry

    lax.fori_loop(0, nck, low_halves, 0)
    t_lo = kth_largest_half(kk - above)
    thr = t_hi * 65536 + (t_lo - I16_MIN)
    need = kk - count(lambda c, k: jnp.where(k > thr, 1.0, 0.0))
    n_eq = count(lambda c, k: jnp.where(k == thr, 1.0, 0.0))
    surplus = jnp.max(jnp.where(thr > KEY_NEG_INF, n_eq - need, 0.0))

    def break_ties():
        def index_step(it, j):
            cand = j + lax.shift_left(jnp.int32(1), idx_bits - 1 - it)
            cnt = count(lambda c, k: jnp.where(k == thr, jnp.where(c * KC + row < cand, 1.0, 0.0), 0.0))
            return jnp.where(cnt < need, cand, j)
        return lax.fori_loop(0, idx_bits, index_step, jnp.zeros((1, QB), I32))

    jlim = lax.cond(surplus > 0.0, break_ties, lambda: jnp.full((1, QB), NO_TIE_LIMIT, I32))

    def write_mask(c, carry):
        k = key_ref[c]
        tie = jnp.where(c * KC + row <= jlim, 0.0, MASK_NEG)
        sel = jnp.where(k > thr, 0.0, jnp.where(k == thr, tie, MASK_NEG))
        mask = jnp.where(k > KEY_NEG_INF, sel, MASK_NEG)
        mb_ref[pl.ds(c * (KC // KS), KC // KS)] = mask.reshape(KC // KS, KS, QB)
        return carry

    lax.fori_loop(0, nck, write_mask, 0)

    qt = qt_ref[0] * ((HEAD_DIM ** -0.5) * LOG2E)
    zero_blk = jnp.zeros((HEAD_DIM, QB), BF16)
    for g in range(N_HEADS // HEAD_GROUP):
        for j in range(HEAD_GROUP):
            h = g * HEAD_GROUP + j
            qh = qt[h * HEAD_DIM:(h + 1) * HEAD_DIM, :].astype(BF16)
            qbd_ref[g, :, j * QB:(j + 1) * QB] = jnp.concatenate(
                [qh if r == j else zero_blk for r in range(HEAD_GROUP)], axis=0)
    m_ref[...] = jnp.full(m_ref.shape, MASK_NEG, F32)
    l_ref[...] = jnp.zeros(l_ref.shape, F32)
    ot_ref[0] = jnp.zeros((D_ATT, QB), F32)
    tiles = lambda x: x.reshape(x.shape[0] // SUBLANES, SUBLANES, QB)
    gw = HEAD_GROUP * HEAD_DIM

    def attend(c, band_sel):
        mbc = mb_ref[c]
        for g in range(N_HEADS // HEAD_GROUP):
            sg = jnp.dot(k_ref[0, c, :, g * gw:(g + 1) * gw], qbd_ref[g], preferred_element_type=F32)
            for j in range(HEAD_GROUP):
                h = g * HEAD_GROUP + j
                hs = slice(h * HEAD_DIM, (h + 1) * HEAD_DIM)
                s = sg[:, j * QB:(j + 1) * QB] + mbc
                if band_sel is not None:
                    s = s + band_ref[band_sel, h]
                s = tiles(s)
                m_old = m_ref[h]
                m_new = jnp.maximum(m_old, jnp.max(jnp.max(s, axis=0), axis=0, keepdims=True))
                alpha = jnp.exp2(m_old - m_new)
                p = jnp.exp2(s - m_new[None])
                l_ref[h] = alpha * l_ref[h] + jnp.sum(jnp.sum(p, axis=0), axis=0, keepdims=True)
                m_ref[h] = m_new
                pv = jnp.dot(vt_ref[0, c, hs, :], p.reshape(KS, QB).astype(BF16), preferred_element_type=F32)
                ot_ref[0, hs, :] = (alpha[None] * tiles(ot_ref[0, hs, :]) + tiles(pv)).reshape(HEAD_DIM, QB)

    def far_step(c, carry):
        attend(c, None)
        return carry

    n_far = jnp.maximum((QB // KS) * i - 1, 0)

    def near_step(c, carry):
        attend(c, c - (QB // KS) * i + 1)
        return carry

    lax.fori_loop(0, n_far, far_step, 0)
    lax.fori_loop(n_far, (QB // KS) * nck, near_step, 0)
    for h in range(N_HEADS):
        hs = slice(h * HEAD_DIM, (h + 1) * HEAD_DIM)
        ot_ref[0, hs, :] = (tiles(ot_ref[0, hs, :]) / l_ref[h][None]).reshape(HEAD_DIM, QB)


def _shifted_bias(rel_bias, buckets):
    far = rel_bias[N_BUCKETS - 1]
    lead = (N_HEADS,) + (1,) * buckets.ndim
    idx = jnp.asarray(buckets)[None]
    out = jnp.zeros((N_HEADS,) + buckets.shape, F32)
    for b in np.unique(buckets):
        if b != N_BUCKETS - 1:
            out = jnp.where(idx == b, (rel_bias[b] - far).reshape(lead), out)
    return out


def _prompt_bands(rel_bias):
    tab = _rel_bucket_table(QB + KS)
    sj = np.arange(KS)[:, None]
    qi = np.arange(QB)[None, :]
    buckets = np.stack([tab[np.maximum(qi - sj - (n - 1) * KS, 0)] for n in range(N_NEAR)])
    return jnp.transpose(_shifted_bias(rel_bias, buckets), (1, 0, 2, 3)) * LOG2E


def _prompt_attention(qt, k, vtc, qit, kw, kwt, bands):
    b, t = qt.shape[0], qt.shape[2]
    nc = t // KC
    ns = t // KS
    kk = min(TOPK_MAX, t // 4)
    kc = k.reshape(b, ns, KS, D_ATT)
    ki = kw.reshape(b, nc, KC, LANES)
    col = lambda n: pl.BlockSpec((1, n, QB), lambda bi, i: (bi, 0, i))
    res = lambda s: pl.BlockSpec((1,) + s, lambda bi, i: (bi, 0, 0, 0))
    w_rows = pl.BlockSpec((1, SUBLANES, QB), lambda bi, i: (bi, D_IDX // SUBLANES, i))
    kern = functools.partial(_prompt_attn_kernel, kk=kk, idx_bits=max(1, (t - 1).bit_length()))
    return pl.pallas_call(
        kern,
        grid=(b, t // QB),
        in_specs=[col(D_ATT), res((ns, KS, D_ATT)), res((ns, D_ATT, KS)),
                  col(N_IDX_HEADS * D_IDX), res((nc, KC, LANES)), w_rows,
                  pl.BlockSpec(bands.shape, lambda bi, i: (0, 0, 0, 0))],
        out_specs=col(D_ATT),
        out_shape=jax.ShapeDtypeStruct((b, D_ATT, t), F32),
        scratch_shapes=[pltpu.VMEM((nc, KC, QB), I32), pltpu.VMEM((nc, KC, QB), I16),
                        pltpu.VMEM((ns, KS, QB), F32),
                        pltpu.VMEM((N_HEADS, SUBLANES, QB), F32), pltpu.VMEM((N_HEADS, SUBLANES, QB), F32),
                        pltpu.VMEM((N_HEADS // HEAD_GROUP, HEAD_GROUP * HEAD_DIM, HEAD_GROUP * QB), BF16)],
        compiler_params=_cparams(2),
        name="prompt_attention",
    )(qt, kc, vtc, qit, ki, kwt, bands)


def _s5_coeffs(lam_re, lam_im, log_step):
    lr = jnp.minimum(lam_re, -1e-4)
    li = lam_im
    dt = jnp.exp(log_step)
    mag = jnp.exp(lr * dt)
    a_re = mag * jnp.cos(li * dt)
    a_im = mag * jnp.sin(li * dt)
    den = lr * lr + li * li
    f_re = ((a_re - 1.0) * lr + a_im * li) / den
    f_im = (a_im * lr - (a_re - 1.0) * li) / den
    return a_re, a_im, f_re, f_im


def _s5_input_drive(u, f_re, f_im, bre_ref, bim_ref):
    w_re = f_re * bre_ref[...] - f_im * bim_ref[...]
    w_im = f_re * bim_ref[...] + f_im * bre_ref[...]
    ub = u.astype(BF16)
    return _bdot(ub, w_re), _bdot(ub, w_im)


def _s5_readout(h_re, h_im, u, gate, cre_ref, cim_ref, d_ref, gw_ref, gb_ref):
    y = _bdot(h_re, cre_ref[...]) - _bdot(h_im, cim_ref[...])
    y = y + d_ref[...] * u
    z = _gelu_tanh(y)
    z = z * _sigmoid(_bdot(z, gw_ref[...]) + gb_ref[...])
    return z * _silu(gate)


def _cmul(ar, ai, br, bi):
    return ar * br - ai * bi, ar * bi + ai * br


def _s5_seq_kernel(x_ref, h0re_ref, h0im_ref, lre_ref, lim_ref, ls_ref, bre_ref, bim_ref,
                   cre_ref, cim_ref, d_ref, gw_ref, gb_ref,
                   o_ref, hre_out, him_out, hre_s, him_s, cre_s, cim_s):
    c = pl.program_id(1)
    tc = x_ref.shape[1]

    @pl.when(c == 0)
    def _():
        cre_s[...] = h0re_ref[0]
        cim_s[...] = h0im_ref[0]

    u = x_ref[0, :, 0:D_SSM]
    gate = x_ref[0, :, D_SSM:]
    a_re, a_im, f_re, f_im = _s5_coeffs(lre_ref[...], lim_ref[...], ls_ref[...])
    bu_re, bu_im = _s5_input_drive(u, f_re, f_im, bre_ref, bim_ref)
    hre_s[...] = bu_re
    him_s[...] = bu_im

    full = lambda x: jnp.broadcast_to(x, (SUBLANES, S5_P))
    pows = [(full(a_re), full(a_im))]
    for _ in range(SUBLANES - 1):
        pows.append(_cmul(pows[-1][0], pows[-1][1], pows[0][0], pows[0][1]))
    row = lax.broadcasted_iota(I32, (SUBLANES, S5_P), 0)
    pw_re, pw_im = pows[SUBLANES - 1]
    for r in range(SUBLANES - 2, -1, -1):
        pw_re = jnp.where(row == r, pows[r][0], pw_re)
        pw_im = jnp.where(row == r, pows[r][1], pw_im)

    def tile(j, carry):
        cr, ci = carry
        sl = pl.ds(pl.multiple_of(j * SUBLANES, SUBLANES), SUBLANES)
        xr = hre_s[sl, :]
        xi = him_s[sl, :]
        for s in (1, 2, 4):
            sr = jnp.where(row >= s, pltpu.roll(xr, s, 0), 0.0)
            si = jnp.where(row >= s, pltpu.roll(xi, s, 0), 0.0)
            pr, pi = _cmul(pows[s - 1][0], pows[s - 1][1], sr, si)
            xr = xr + pr
            xi = xi + pi
        pr, pi = _cmul(pw_re, pw_im, cr, ci)
        xr = xr + pr
        xi = xi + pi
        hre_s[sl, :] = xr
        him_s[sl, :] = xi
        return xr[SUBLANES - 1:, :], xi[SUBLANES - 1:, :]

    cr, ci = lax.fori_loop(0, tc // SUBLANES, tile, (cre_s[...], cim_s[...]))
    cre_s[...] = cr
    cim_s[...] = ci
    hre_out[0] = cr
    him_out[0] = ci
    o_ref[0] = _s5_readout(hre_s[...], him_s[...], u, gate, cre_ref, cim_ref, d_ref, gw_ref, gb_ref)


def _s5_step_kernel(x_ref, h0re_ref, h0im_ref, lre_ref, lim_ref, ls_ref, bre_ref, bim_ref,
                    cre_ref, cim_ref, d_ref, gw_ref, gb_ref, o_ref, hre_out, him_out):
    u = x_ref[:, 0:D_SSM]
    gate = x_ref[:, D_SSM:]
    a_re, a_im, f_re, f_im = _s5_coeffs(lre_ref[...], lim_ref[...], ls_ref[...])
    bu_re, bu_im = _s5_input_drive(u, f_re, f_im, bre_ref, bim_ref)
    pr, pi = _cmul(a_re, a_im, h0re_ref[...], h0im_ref[...])
    h_re = bu_re + pr
    h_im = bu_im + pi
    hre_out[...] = h_re
    him_out[...] = h_im
    o_ref[...] = _s5_readout(h_re, h_im, u, gate, cre_ref, cim_ref, d_ref, gw_ref, gb_ref)


def _s5_params(lp):
    eye = jnp.eye(N_SSM_GROUPS, dtype=F32)

    def in_bd(b):
        return (jnp.transpose(b, (0, 2, 1))[:, :, None, :] * eye[:, None, :, None]).reshape(D_SSM, S5_P)

    def out_bd(c):
        return (jnp.transpose(c, (0, 2, 1))[:, :, None, :] * eye[:, None, :, None]).reshape(S5_P, D_SSM)

    ls = jnp.broadcast_to(lp['log_step'][:, None], (N_SSM_GROUPS, SSM_STATE))
    return (lp['lam_re'].reshape(1, S5_P), lp['lam_im'].reshape(1, S5_P), ls.reshape(1, S5_P),
            in_bd(lp['b_re']), in_bd(lp['b_im']),
            out_bd(lp['c_re']).astype(BF16), out_bd(lp['c_im']).astype(BF16),
            lp['d_skip'].reshape(1, D_SSM), lp['glu_w'].astype(BF16), lp['glu_b'].reshape(1, D_SSM))


def _s5_seq(x, h0_re, h0_im, params):
    b, t = x.shape[0], x.shape[1]
    tc = min(SCAN_CHUNK, t)
    full = lambda a: pl.BlockSpec(a.shape, lambda bi, c: (0,) * a.ndim)
    st = pl.BlockSpec((1, 1, S5_P), lambda bi, c: (bi, 0, 0))
    return pl.pallas_call(
        _s5_seq_kernel,
        grid=(b, t // tc),
        in_specs=[pl.BlockSpec((1, tc, 2 * D_SSM), lambda bi, c: (bi, c, 0)), st, st]
                 + [full(a) for a in params],
        out_specs=[pl.BlockSpec((1, tc, D_SSM), lambda bi, c: (bi, c, 0)), st, st],
        out_shape=[jax.ShapeDtypeStruct((b, t, D_SSM), F32),
                   jax.ShapeDtypeStruct((b, 1, S5_P), F32),
                   jax.ShapeDtypeStruct((b, 1, S5_P), F32)],
        scratch_shapes=[pltpu.VMEM((tc, S5_P), F32), pltpu.VMEM((tc, S5_P), F32),
                        pltpu.VMEM((1, S5_P), F32), pltpu.VMEM((1, S5_P), F32)],
        compiler_params=_cparams(2),
        name="s5_sequence",
    )(x, h0_re, h0_im, *params)


def _s5_step(x, h0_re, h0_im, params):
    b = x.shape[0]
    return pl.pallas_call(
        _s5_step_kernel,
        out_shape=[jax.ShapeDtypeStruct((b, D_SSM), F32),
                   jax.ShapeDtypeStruct((b, S5_P), F32),
                   jax.ShapeDtypeStruct((b, S5_P), F32)],
        compiler_params=pltpu.CompilerParams(vmem_limit_bytes=VMEM_LIMIT),
        name="s5_step",
    )(x, h0_re, h0_im, *params)


def _lru_gates(xc, wa_ref, ba_ref, wx_ref, bx_ref, lam_ref):
    xb = xc.astype(BF16)
    r = _sigmoid(_bdot(xb, wa_ref[...]) + ba_ref[...])
    g = _sigmoid(_bdot(xb, wx_ref[...]) + bx_ref[...])
    log_a = -LRU_C * r * _softplus(-lam_ref[...])
    a = jnp.exp(log_a)
    b = jnp.sqrt(-jnp.tanh(log_a) * (a * a + 1.0)) * (g * xc)
    return a, b


def _lru_seq_kernel(x_ref, conv0_ref, h0_ref, cw_ref, cb_ref, wa_ref, ba_ref, wx_ref, bx_ref, lam_ref,
                    o_ref, h_out, conv_out, ubuf, a_s, b_s, c_s):
    c = pl.program_id(1)
    tc = x_ref.shape[1]
    hist = CONV_W - 1

    @pl.when(c == 0)
    def _():
        ubuf[0:SUBLANES, :] = conv0_ref[0]
        c_s[...] = h0_ref[0]

    u = x_ref[0, :, 0:D_LRU]
    gate = x_ref[0, :, D_LRU:]
    ubuf[SUBLANES:, :] = u
    xc = cb_ref[...] + u * cw_ref[hist:hist + 1, :]
    for j in range(hist):
        xc = xc + ubuf[SUBLANES - hist + j:SUBLANES - hist + j + tc, :] * cw_ref[j:j + 1, :]
    tail = ubuf[tc:tc + SUBLANES, :]
    ubuf[0:SUBLANES, :] = tail
    conv_out[0] = tail

    a, b = _lru_gates(xc, wa_ref, ba_ref, wx_ref, bx_ref, lam_ref)
    a_s[...] = a
    b_s[...] = b
    row = lax.broadcasted_iota(I32, (SUBLANES, D_LRU), 0)

    def tile(j, carry):
        sl = pl.ds(pl.multiple_of(j * SUBLANES, SUBLANES), SUBLANES)
        av = a_s[sl, :]
        bv = b_s[sl, :]
        for s in (1, 2, 4):
            a_sh = jnp.where(row >= s, pltpu.roll(av, s, 0), 1.0)
            b_sh = jnp.where(row >= s, pltpu.roll(bv, s, 0), 0.0)
            bv = av * b_sh + bv
            av = av * a_sh
        h = bv + av * carry
        b_s[sl, :] = h
        return h[SUBLANES - 1:, :]

    carry = lax.fori_loop(0, tc // SUBLANES, tile, c_s[...])
    c_s[...] = carry
    h_out[0] = carry
    o_ref[0] = b_s[...] * _silu(gate)


def _lru_step_kernel(x_ref, c0_ref, c1_ref, c2_ref, h0_ref, cw_ref, cb_ref, wa_ref, ba_ref, wx_ref,
                     bx_ref, lam_ref, o_ref, h_out):
    u = x_ref[:, 0:D_LRU]
    gate = x_ref[:, D_LRU:]
    xc = (cb_ref[...] + c0_ref[...] * cw_ref[0:1, :] + c1_ref[...] * cw_ref[1:2, :]
          + c2_ref[...] * cw_ref[2:3, :] + u * cw_ref[3:4, :])
    a, b = _lru_gates(xc, wa_ref, ba_ref, wx_ref, bx_ref, lam_ref)
    h = a * h0_ref[...] + b
    h_out[...] = h
    o_ref[...] = h * _silu(gate)


def _lru_params(lp):
    eye = jnp.eye(N_LRU_BLOCKS, dtype=F32)
    bd = lambda w: (w[:, :, None, :] * eye[:, None, :, None]).reshape(D_LRU, D_LRU).astype(BF16)
    cw = jnp.concatenate([lp['conv_w'], jnp.zeros((SUBLANES - CONV_W, D_LRU), F32)], axis=0)
    return (cw, lp['conv_b'].reshape(1, D_LRU), bd(lp['wa']), lp['ba'].reshape(1, D_LRU),
            bd(lp['wx']), lp['bx'].reshape(1, D_LRU), lp['lam'].reshape(1, D_LRU))


def _lru_seq(x, conv0, h0, params):
    b, t = x.shape[0], x.shape[1]
    tc = min(SCAN_CHUNK, t)
    full = lambda a: pl.BlockSpec(a.shape, lambda bi, c: (0,) * a.ndim)
    return pl.pallas_call(
        _lru_seq_kernel,
        grid=(b, t // tc),
        in_specs=[pl.BlockSpec((1, tc, 2 * D_LRU), lambda bi, c: (bi, c, 0)),
                  pl.BlockSpec((1, SUBLANES, D_LRU), lambda bi, c: (bi, 0, 0)),
                  pl.BlockSpec((1, 1, D_LRU), lambda bi, c: (bi, 0, 0))]
                 + [full(a) for a in params],
        out_specs=[pl.BlockSpec((1, tc, D_LRU), lambda bi, c: (bi, c, 0)),
                   pl.BlockSpec((1, 1, D_LRU), lambda bi, c: (bi, 0, 0)),
                   pl.BlockSpec((1, SUBLANES, D_LRU), lambda bi, c: (bi, 0, 0))],
        out_shape=[jax.ShapeDtypeStruct((b, t, D_LRU), F32),
                   jax.ShapeDtypeStruct((b, 1, D_LRU), F32),
                   jax.ShapeDtypeStruct((b, SUBLANES, D_LRU), F32)],
        scratch_shapes=[pltpu.VMEM((tc + SUBLANES, D_LRU), F32), pltpu.VMEM((tc, D_LRU), F32),
                        pltpu.VMEM((tc, D_LRU), F32), pltpu.VMEM((1, D_LRU), F32)],
        compiler_params=_cparams(2),
        name="lru_sequence",
    )(x, conv0, h0, *params)


def _lru_step(x, conv0, h0, params):
    b = x.shape[0]
    return pl.pallas_call(
        _lru_step_kernel,
        out_shape=[jax.ShapeDtypeStruct((b, D_LRU), F32), jax.ShapeDtypeStruct((b, D_LRU), F32)],
        compiler_params=pltpu.CompilerParams(vmem_limit_bytes=VMEM_LIMIT),
        name="lru_step",
    )(x, conv0[:, 0], conv0[:, 1], conv0[:, 2], h0, *params)


def _dec_score_kernel(pt_ref, qi_ref, w_ref, *refs):
    pages, o_ref = refs[:-1], refs[-1]
    qi = qi_ref[0].astype(BF16)
    w = w_ref[0] * (N_IDX_HEADS ** -0.5)
    for g, page in enumerate(pages):
        d = jnp.dot(qi, page[0, 0].astype(BF16), preferred_element_type=F32) * (D_IDX ** -0.5)
        o_ref[0, 0, g:g + 1, :] = jnp.sum(w * jnp.maximum(d, 0.0), axis=0, keepdims=True)


def _dec_scores(q_idx, w_idx, kidx_t, page_table, layer):
    b = q_idx.shape[0]
    n_pages = page_table.shape[1]
    page = kidx_t.shape[3]
    g = math.gcd(SCORE_PAGES_PER_STEP, n_pages)
    page_spec = lambda j: pl.BlockSpec(
        (1, 1, D_IDX, page), lambda bi, p, pt: (layer, pt[bi * n_pages + p * g + j], 0, 0))
    out = pl.pallas_call(
        _dec_score_kernel,
        grid_spec=pltpu.PrefetchScalarGridSpec(
            num_scalar_prefetch=1,
            grid=(b, n_pages // g),
            in_specs=[pl.BlockSpec((1, N_IDX_HEADS, D_IDX), lambda bi, p, pt: (bi, 0, 0)),
                      pl.BlockSpec((1, N_IDX_HEADS, 1), lambda bi, p, pt: (bi, 0, 0))]
                     + [page_spec(j) for j in range(g)],
            out_specs=pl.BlockSpec((1, 1, g, page), lambda bi, p, pt: (bi, p, 0, 0)),
        ),
        out_shape=jax.ShapeDtypeStruct((b, n_pages // g, g, page), F32),
        compiler_params=_cparams(2),
        name="decode_scores",
    )(page_table.reshape(-1), q_idx.reshape(b, N_IDX_HEADS, D_IDX),
      w_idx.reshape(b, N_IDX_HEADS, 1), *([kidx_t] * g))
    return out.reshape(b, n_pages * page)


def _dec_select_kernel(s_ref, qi_ref, kn_ref, w_ref, mb_ref, mbn_ref, *, kk, idx_bits):
    b, past = s_ref.shape
    n = past + LANES
    qi = qi_ref[...]
    kn = kn_ref[...]
    w = w_ref[...] * (N_IDX_HEADS ** -0.5)
    s_new = jnp.zeros((b, 1), F32)
    for h in range(N_IDX_HEADS):
        d = jnp.sum(qi[:, h * D_IDX:(h + 1) * D_IDX] * kn, axis=1, keepdims=True) * (D_IDX ** -0.5)
        s_new = s_new + w[:, h:h + 1] * jnp.maximum(d, 0.0)
    lane = lax.broadcasted_iota(I32, (1, LANES), 1)
    tail = jnp.where(lane == 0, _sortable(jnp.broadcast_to(s_new, (b, LANES))), KEY_NEG_INF)
    keys = jnp.concatenate([_sortable(s_ref[...]), tail], axis=1)
    pos = lax.broadcasted_iota(I32, (1, n), 1)

    def count(m):
        return jnp.sum(m, axis=1, keepdims=True)

    def value_step(it, t):
        cand = t + lax.shift_left(jnp.int32(1), 31 - it)
        return jnp.where(count(jnp.where(keys >= cand, 1.0, 0.0)) >= kk, cand, t)

    thr = lax.fori_loop(0, 32, value_step, jnp.full((b, 1), INT_MIN, I32))
    need = kk - count(jnp.where(keys > thr, 1.0, 0.0))

    def index_step(it, j):
        cand = j + lax.shift_left(jnp.int32(1), idx_bits - 1 - it)
        c = count(jnp.where(keys == thr, jnp.where(pos < cand, 1.0, 0.0), 0.0))
        return jnp.where(c < need, cand, j)

    jlim = lax.fori_loop(0, idx_bits, index_step, jnp.zeros((b, 1), I32))
    tie = jnp.where(pos <= jlim, 0.0, MASK_NEG)
    sel = jnp.where(keys > thr, 0.0, jnp.where(keys == thr, tie, MASK_NEG))
    sel = jnp.where(keys > KEY_NEG_INF, sel, MASK_NEG)
    mb_ref[...] = sel[:, :past]
    mbn_ref[...] = sel[:, past:]


def _dec_select(scores, q_idx, k_idx_new, w_idx, kk):
    b, past = scores.shape
    kern = functools.partial(_dec_select_kernel, kk=kk, idx_bits=max(1, (past + LANES - 1).bit_length()))
    return pl.pallas_call(
        kern,
        out_shape=[jax.ShapeDtypeStruct((b, past), F32), jax.ShapeDtypeStruct((b, LANES), F32)],
        compiler_params=pltpu.CompilerParams(vmem_limit_bytes=VMEM_LIMIT),
        name="decode_select",
    )(scores, q_idx, k_idx_new, w_idx)


def _dec_attn_kernel(pt_ref, qb_ref, q_ref, kn_ref, vn_ref, mb_ref, mbn_ref, bias_ref, biasn_ref, *refs):
    g = (len(refs) - 4) // 2
    k_refs, v_refs = refs[:g], refs[g:2 * g]
    o_ref, m_s, l_s, acc_s = refs[2 * g:]
    step = pl.program_id(1)
    scale = HEAD_DIM ** -0.5

    @pl.when(step == 0)
    def _():
        m_s[...] = jnp.full(m_s.shape, MASK_NEG, F32)
        l_s[...] = jnp.zeros(l_s.shape, F32)
        acc_s[...] = jnp.zeros(acc_s.shape, F32)

    qb = qb_ref[0]
    s = jnp.concatenate([jnp.sum(kr[0, 0] * qb, axis=1) for kr in k_refs], axis=1)
    s = s * scale + bias_ref[...] + mb_ref[0]
    m_old = m_s[...]
    m_new = jnp.maximum(m_old, jnp.max(s, axis=1, keepdims=True))
    alpha = jnp.exp(m_old - m_new)
    p = jnp.exp(s - m_new)
    l_s[...] = alpha * l_s[...] + jnp.sum(p, axis=1, keepdims=True)
    m_s[...] = m_new
    page = qb.shape[2]
    for h in range(N_HEADS):
        upd = acc_s[h] * alpha[h:h + 1, :]
        for j, vr in enumerate(v_refs):
            upd = upd + p[h:h + 1, j * page:(j + 1) * page] * vr[0, 0, h]
        acc_s[h] = upd

    @pl.when(step == pl.num_programs(1) - 1)
    def _():
        ones = jnp.ones((1, page), F32)
        ctx = jnp.concatenate(
            [lax.dot_general(ones, acc_s[h], (((1,), (1,)), ((), ())), precision=HI,
                             preferred_element_type=F32) for h in range(N_HEADS)], axis=0)
        s_new = (jnp.sum(q_ref[0] * kn_ref[0], axis=1, keepdims=True) * scale
                 + biasn_ref[...] + mbn_ref[0][:, 0:1])
        m_f = jnp.maximum(m_s[...], s_new)
        a = jnp.exp(m_s[...] - m_f)
        pn = jnp.exp(s_new - m_f)
        o_ref[0] = (a * ctx + pn * vn_ref[0]) / (a * l_s[...] + pn)


def _dec_attention(q, k_new, v_new, mb, mb_new, bias, bias_new, ck_t, cv_t, page_table, layer):
    b = q.shape[0]
    n_pages = page_table.shape[1]
    page = ck_t.shape[4]
    g = math.gcd(ATTN_PAGES_PER_STEP, n_pages)
    steps = n_pages // g
    q3 = q.reshape(b, N_HEADS, HEAD_DIM)
    qb = jnp.broadcast_to(q3[:, :, :, None], (b, N_HEADS, HEAD_DIM, page))
    row = lambda: pl.BlockSpec((1, N_HEADS, HEAD_DIM), lambda bi, p, pt: (bi, 0, 0))
    page_spec = lambda j: pl.BlockSpec(
        (1, 1, N_HEADS, HEAD_DIM, page), lambda bi, p, pt: (layer, pt[bi * n_pages + p * g + j], 0, 0, 0))
    return pl.pallas_call(
        _dec_attn_kernel,
        grid_spec=pltpu.PrefetchScalarGridSpec(
            num_scalar_prefetch=1,
            grid=(b, steps),
            in_specs=[pl.BlockSpec((1, N_HEADS, HEAD_DIM, page), lambda bi, p, pt: (bi, 0, 0, 0)),
                      row(), row(), row(),
                      pl.BlockSpec((1, 1, g * page), lambda bi, p, pt: (bi, 0, p)),
                      pl.BlockSpec((1, 1, LANES), lambda bi, p, pt: (bi, 0, 0)),
                      pl.BlockSpec((N_HEADS, g * page), lambda bi, p, pt: (0, p)),
                      pl.BlockSpec((N_HEADS, 1), lambda bi, p, pt: (0, 0))]
                     + [page_spec(j) for j in range(g)] * 2,
            out_specs=row(),
            scratch_shapes=[pltpu.VMEM((N_HEADS, 1), F32), pltpu.VMEM((N_HEADS, 1), F32),
                            pltpu.VMEM((N_HEADS, HEAD_DIM, page), F32)],
        ),
        out_shape=jax.ShapeDtypeStruct((b, N_HEADS, HEAD_DIM), F32),
        compiler_params=_cparams(2),
        name="decode_attention",
    )(page_table.reshape(-1), qb, q3, k_new.reshape(b, N_HEADS, HEAD_DIM),
      v_new.reshape(b, N_HEADS, HEAD_DIM), mb.reshape(b, 1, n_pages * page),
      mb_new.reshape(b, 1, LANES), bias, bias_new, *([ck_t] * g), *([cv_t] * g))


def _decode_bias(rel_bias, past):
    tab = _rel_bucket_table(past + 1)
    bias = _shifted_bias(rel_bias, tab[past - np.arange(past)])
    return bias, (rel_bias[0] - rel_bias[N_BUCKETS - 1]).reshape(N_HEADS, 1)


def kernel(x_prompt, x_sample, cache_k, cache_v, cache_kidx, state_s5_re, state_s5_im, state_lru_h,
           state_lru_conv, page_table, w_in, w_out, ln_g, ln_b, rel_bias, s5_lam_re, s5_lam_im,
           s5_log_step, s5_b_re, s5_b_im, s5_c_re, s5_c_im, s5_d, glu_w, glu_b, lru_conv_w, lru_conv_b,
           lru_wa, lru_ba, lru_wx, lru_bx, lru_lam):
    depth = w_in.shape[0]
    bp, t = x_prompt.shape[0], x_prompt.shape[1]
    bs = x_sample.shape[0]
    n_pages = page_table.shape[1]
    page = cache_k.shape[2]
    past = n_pages * page
    kk_s = min(TOPK_MAX, (past + 1) // 4)
    bands = _prompt_bands(rel_bias)
    dec_bias, dec_bias_new = _decode_bias(rel_bias, past)
    hist = CONV_W - 1
    ck_t = jnp.transpose(cache_k, (0, 1, 3, 4, 2))
    cv_t = jnp.transpose(cache_v, (0, 1, 3, 4, 2))
    kidx_t = jnp.transpose(cache_kidx, (0, 1, 3, 2))

    xp = x_prompt.reshape(bp * t, D_MODEL)
    xs = x_sample.reshape(bs, D_MODEL)
    zero_state = jnp.zeros((bp, 1, S5_P), F32)
    zero_h = jnp.zeros((bp, 1, D_LRU), F32)
    zero_conv = jnp.zeros((bp, SUBLANES, D_LRU), F32)
    new_p = [[] for _ in range(7)]
    new_s = [[] for _ in range(7)]

    for l in range(depth):
        w_packed = _pack_w_in(w_in[l])
        w_out_l = w_out[l].astype(BF16)
        s5p = _s5_params({'lam_re': s5_lam_re[l], 'lam_im': s5_lam_im[l], 'log_step': s5_log_step[l],
                          'b_re': s5_b_re[l], 'b_im': s5_b_im[l], 'c_re': s5_c_re[l], 'c_im': s5_c_im[l],
                          'd_skip': s5_d[l], 'glu_w': glu_w[l], 'glu_b': glu_b[l]})
        lrup = _lru_params({'conv_w': lru_conv_w[l], 'conv_b': lru_conv_b[l], 'wa': lru_wa[l],
                            'ba': lru_ba[l], 'wx': lru_wx[l], 'bx': lru_bx[l], 'lam': lru_lam[l]})
        lng = ln_g[l].reshape(1, D_MODEL)
        lnb = ln_b[l].reshape(1, D_MODEL)

        hp = _in_proj_prompt(xp.reshape(bp, t, D_MODEL), w_packed)
        o_att_t = _prompt_attention(hp['qT'], hp['k'], hp['vtc'], hp['qiT'], hp['kw'], hp['kwT'], bands)
        o_att = jnp.transpose(o_att_t, (0, 2, 1)).reshape(bp * t, D_ATT)
        o_ssm, s5re, s5im = _s5_seq(hp['ssm'], zero_state, zero_state, s5p)
        o_lru, lruh, conv = _lru_seq(hp['lru'], zero_conv, zero_h, lrup)
        flat = lambda a: a.reshape(bp * t, a.shape[-1])
        xp_new = _out_ln(xp, o_att, flat(hp['g']), flat(o_ssm), flat(o_lru), w_out_l, lng, lnb,
                         "out_ln_prompt")
        for lst, a in zip(new_p, (hp['kT'].reshape(bp, N_HEADS, HEAD_DIM, t),
                                  hp['vT'].reshape(bp, N_HEADS, HEAD_DIM, t),
                                  hp['kwT'][:, :D_IDX, :],
                                  s5re.reshape(bp, N_SSM_GROUPS, SSM_STATE),
                                  s5im.reshape(bp, N_SSM_GROUPS, SSM_STATE),
                                  lruh.reshape(bp, D_LRU), conv[:, SUBLANES - hist:, :])):
            lst.append(a)
        xp = xp_new

        hs = _in_proj(xs, w_packed, "in_proj_sample")
        kidx_s = hs['kw'][:, :D_IDX]
        widx_s = hs['kw'][:, D_IDX:D_IDX + N_IDX_HEADS]
        scores = _dec_scores(hs['qi'], widx_s, kidx_t, page_table, l)
        mb, mb_new = _dec_select(scores, hs['qi'], kidx_s, widx_s, kk_s)
        o_att_s = _dec_attention(hs['q'], hs['k'], hs['v'], mb, mb_new, dec_bias, dec_bias_new,
                                 ck_t, cv_t, page_table, l)
        o_ssm_s, s5re_s, s5im_s = _s5_step(hs['ssm'], state_s5_re[l].reshape(bs, S5_P),
                                           state_s5_im[l].reshape(bs, S5_P), s5p)
        conv0 = state_lru_conv[l]
        o_lru_s, lruh_s = _lru_step(hs['lru'], conv0, state_lru_h[l], lrup)
        conv_s = jnp.concatenate([conv0[:, 1:], hs['lru'][:, None, :D_LRU]], axis=1)
        xs_new = _out_ln(xs, o_att_s.reshape(bs, D_ATT), hs['g'], o_ssm_s, o_lru_s, w_out_l, lng, lnb,
                         "out_ln_sample")
        for lst, a in zip(new_s, (hs['k'].reshape(bs, 1, N_HEADS, HEAD_DIM),
                                  hs['v'].reshape(bs, 1, N_HEADS, HEAD_DIM),
                                  kidx_s.reshape(bs, 1, D_IDX),
                                  s5re_s.reshape(bs, N_SSM_GROUPS, SSM_STATE),
                                  s5im_s.reshape(bs, N_SSM_GROUPS, SSM_STATE),
                                  lruh_s, conv_s)):
            lst.append(a)
        xs = xs_new

    outs_p = [jnp.stack(a) for a in new_p]
    outs_p[0] = jnp.transpose(outs_p[0], (0, 1, 4, 2, 3))
    outs_p[1] = jnp.transpose(outs_p[1], (0, 1, 4, 2, 3))
    outs_p[2] = jnp.transpose(outs_p[2], (0, 1, 3, 2))
    outs_s = [jnp.stack(a) for a in new_s]
    return (xp.reshape(bp, t, D_MODEL), xs.reshape(bs, 1, D_MODEL), *outs_p, *outs_s)
```

```python
import functools
import math

import numpy as np
import jax
import jax.numpy as jnp
from jax import lax
from jax.experimental import pallas as pl
from jax.experimental.pallas import tpu as pltpu

F32 = jnp.float32
BF16 = jnp.bfloat16
I32 = jnp.int32
I16 = jnp.int16
I16_MIN = -2 ** 15
I16_MAX = 2 ** 15 - 1
HI = lax.Precision.HIGHEST

D_MODEL = 1024
D_ATT = 512
HEAD_DIM = 64
N_HEADS = 8
N_IDX_HEADS = 4
D_IDX = 64
TOPK_MAX = 256
N_BUCKETS = 32
MAX_DISTANCE = 128
D_SSM = 256
SSM_CH = 16
N_SSM_GROUPS = 16
SSM_STATE = 64
S5_P = N_SSM_GROUPS * SSM_STATE
D_LRU = 256
N_LRU_BLOCKS = 4
CONV_W = 4
LRU_C = 8.0
LN_EPS = 1e-5
DEPTH = 4
DEEPNORM_ALPHA = (2.0 * DEPTH) ** 0.25
LOG2E = math.log2(math.e)

SUBLANES = 8
LANES = 128
VMEM_LIMIT = 56 * 1024 * 1024

QB = 256
KC = 256
KS = 256
N_NEAR = QB // KS + 1
HEAD_GROUP = 4
SCAN_CHUNK = 512
SCORE_PAGES_PER_STEP = 32
ATTN_PAGES_PER_STEP = 16
MASK_NEG = -1e30
INT_MIN = -2 ** 31
KEY_NEG_INF = -2139095041
NO_TIE_LIMIT = 2 ** 30


def _cparams(n_axes, flags=None):
    return pltpu.CompilerParams(dimension_semantics=("arbitrary",) * n_axes,
                                vmem_limit_bytes=VMEM_LIMIT, flags=flags)


def _bdot(a, b):
    return jnp.dot(a.astype(BF16), b.astype(BF16), preferred_element_type=F32)


def _sigmoid(x):
    return 1.0 / (1.0 + jnp.exp(-x))


def _silu(x):
    return x * _sigmoid(x)


def _gelu_tanh(x):
    return 0.5 * x * (1.0 + jnp.tanh(math.sqrt(2.0 / math.pi) * (x + 0.044715 * (x * x * x))))


def _softplus(x):
    return jnp.maximum(x, 0.0) + jnp.log1p(jnp.exp(-jnp.abs(x)))


def _sortable(x):
    bits = pltpu.bitcast(x, I32)
    return bits ^ ((bits >> 31) & 0x7FFFFFFF)


def _rel_bucket_table(n):
    d = np.arange(n)
    exact = N_BUCKETS // 2
    far = exact + (np.log(np.maximum(d, exact).astype(np.float32) / exact)
                   / math.log(MAX_DISTANCE / exact) * (N_BUCKETS - exact)).astype(np.int32)
    return np.where(d < exact, d, np.minimum(far, N_BUCKETS - 1)).astype(np.int32)


IN_WIDTHS = dict(q=D_ATT, k=D_ATT, v=D_ATT, g=D_ATT, qi=N_IDX_HEADS * D_IDX, kw=LANES,
                 ssm=2 * D_SSM, lru=2 * D_LRU)


def _in_proj_kernel(x_ref, w_ref, *o_refs):
    x = x_ref[...].astype(BF16)
    off = 0
    for o_ref in o_refs:
        n = o_ref.shape[1]
        o_ref[...] = jnp.dot(x, w_ref[:, off:off + n], preferred_element_type=F32)
        off += n


def _pack_w_in(w_in):
    widths = (D_ATT, D_ATT, D_ATT, D_ATT, N_IDX_HEADS * D_IDX, D_IDX, N_IDX_HEADS,
              D_SSM, D_SSM, D_LRU, D_LRU)
    cuts = np.concatenate([[0], np.cumsum(widths)])
    pad = jnp.zeros((D_MODEL, LANES - D_IDX - N_IDX_HEADS), w_in.dtype)
    return jnp.concatenate([w_in[:, :cuts[7]], pad, w_in[:, cuts[7]:]], axis=1).astype(BF16)


def _in_proj(x2d, w_packed, name):
    m = x2d.shape[0]
    tm = min(512, m)
    n_all = w_packed.shape[1]
    outs = pl.pallas_call(
        _in_proj_kernel,
        grid=(m // tm,),
        in_specs=[pl.BlockSpec((tm, D_MODEL), lambda i: (i, 0)),
                  pl.BlockSpec((D_MODEL, n_all), lambda i: (0, 0))],
        out_specs=[pl.BlockSpec((tm, n), lambda i: (i, 0)) for n in IN_WIDTHS.values()],
        out_shape=[jax.ShapeDtypeStruct((m, n), F32) for n in IN_WIDTHS.values()],
        compiler_params=_cparams(1),
        name=name,
    )(x2d, w_packed)
    return dict(zip(IN_WIDTHS.keys(), outs))


IN_T_GROUPS = ('q', 'k', 'v', 'qi', 'kw')
IN_N_GROUPS = ('k', 'g', 'kw', 'ssm', 'lru')
IN_N_DTYPES = dict(k=BF16, g=F32, kw=BF16, ssm=F32, lru=F32)


def _in_proj_prompt_kernel(x_ref, wt_ref, wn_ref, *o_refs):
    nt_refs = o_refs[:len(IN_T_GROUPS)]
    vtc_ref = o_refs[len(IN_T_GROUPS)]
    nn_refs = o_refs[len(IN_T_GROUPS) + 1:]
    x = x_ref[0].astype(BF16)
    off = 0
    for name, o_ref in zip(IN_T_GROUPS, nt_refs):
        n = o_ref.shape[1]
        res = lax.dot_general(wt_ref[off:off + n, :], x, (((1,), (1,)), ((), ())),
                              preferred_element_type=F32)
        o_ref[0] = res
        if name == 'v':
            for j in range(vtc_ref.shape[1]):
                vtc_ref[0, j] = res[:, j * KS:(j + 1) * KS].astype(BF16)
        off += n
    off = 0
    for o_ref in nn_refs:
        n = o_ref.shape[2]
        o_ref[0] = jnp.dot(x, wn_ref[:, off:off + n], preferred_element_type=F32).astype(o_ref.dtype)
        off += n


def _in_proj_prompt(x, w_packed):
    b, t = x.shape[0], x.shape[1]
    tm = min(512, t)
    cols = {}
    off = 0
    for name, n in IN_WIDTHS.items():
        cols[name] = (off, off + n)
        off += n
    pick = lambda names: jnp.concatenate([w_packed[:, cols[n][0]:cols[n][1]] for n in names], axis=1)
    wt = jnp.transpose(pick(IN_T_GROUPS))
    wn = pick(IN_N_GROUPS)
    t_spec = lambda n: pl.BlockSpec((1, n, tm), lambda bi, j: (bi, 0, j))
    n_spec = lambda n: pl.BlockSpec((1, tm, n), lambda bi, j: (bi, j, 0))
    outs = pl.pallas_call(
        _in_proj_prompt_kernel,
        grid=(b, t // tm),
        in_specs=[n_spec(D_MODEL),
                  pl.BlockSpec(wt.shape, lambda bi, j: (0, 0)),
                  pl.BlockSpec(wn.shape, lambda bi, j: (0, 0))],
        out_specs=[t_spec(IN_WIDTHS[n]) for n in IN_T_GROUPS]
                  + [pl.BlockSpec((1, tm // KS, D_ATT, KS), lambda bi, j: (bi, j, 0, 0))]
                  + [n_spec(IN_WIDTHS[n]) for n in IN_N_GROUPS],
        out_shape=[jax.ShapeDtypeStruct((b, IN_WIDTHS[n], t), F32) for n in IN_T_GROUPS]
                  + [jax.ShapeDtypeStruct((b, t // KS, D_ATT, KS), BF16)]
                  + [jax.ShapeDtypeStruct((b, t, IN_WIDTHS[n]), IN_N_DTYPES[n]) for n in IN_N_GROUPS],
        compiler_params=_cparams(2),
        name="in_proj_prompt",
    )(x, wt, wn)
    res = {n + 'T': o for n, o in zip(IN_T_GROUPS, outs)}
    res['vtc'] = outs[len(IN_T_GROUPS)]
    res.update(zip(IN_N_GROUPS, outs[len(IN_T_GROUPS) + 1:]))
    return res


def _out_ln_kernel(x_ref, att_ref, gatt_ref, ssm_ref, lru_ref, w_ref, g_ref, b_ref, o_ref):
    att = att_ref[...] * _silu(gatt_ref[...])
    out = _bdot(att, w_ref[0:D_ATT, :])
    out += _bdot(ssm_ref[...], w_ref[D_ATT:D_ATT + D_SSM, :])
    out += _bdot(lru_ref[...], w_ref[D_ATT + D_SSM:, :])
    y = DEEPNORM_ALPHA * x_ref[...] + out
    mu = jnp.mean(y, axis=-1, keepdims=True)
    yc = y - mu
    var = jnp.mean(yc * yc, axis=-1, keepdims=True)
    o_ref[...] = yc * lax.rsqrt(var + LN_EPS) * g_ref[...] + b_ref[...]


def _out_ln(x, att, gatt, ssm, lru, w_out, ln_g, ln_b, name):
    m = x.shape[0]
    tm = min(512, m)
    row = lambda n: pl.BlockSpec((tm, n), lambda i: (i, 0))
    full = lambda a: pl.BlockSpec(a.shape, lambda i: (0,) * a.ndim)
    return pl.pallas_call(
        _out_ln_kernel,
        grid=(m // tm,),
        in_specs=[row(D_MODEL), row(D_ATT), row(D_ATT), row(D_SSM), row(D_LRU),
                  full(w_out), full(ln_g), full(ln_b)],
        out_specs=row(D_MODEL),
        out_shape=jax.ShapeDtypeStruct((m, D_MODEL), F32),
        compiler_params=_cparams(1),
        name=name,
    )(x, att, gatt, ssm, lru, w_out, ln_g, ln_b)


def _prompt_attn_kernel(qt_ref, k_ref, vt_ref, qit_ref, ki_ref, wt_ref, band_ref, o_ref,
                        key_ref, half_ref, mb_ref, m_ref, l_ref, qbd_ref, acc_ref, qcat_ref, *, kk, idx_bits):
    i = pl.program_id(1)
    nck = i + 1
    qit = qit_ref[0].astype(BF16)
    zero_rows = jnp.zeros((LANES - D_IDX, QB), BF16)
    for h in range(N_IDX_HEADS):
        qcat_ref[:, h * QB:(h + 1) * QB] = jnp.concatenate(
            [qit[h * D_IDX:(h + 1) * D_IDX, :], zero_rows], axis=0)
    w = wt_ref[0] * (N_IDX_HEADS ** -0.5 * D_IDX ** -0.5)
    qpos = i * QB + lax.broadcasted_iota(I32, (1, QB), 1)
    row = lax.broadcasted_iota(I32, (KC, 1), 0)

    def score_chunk(c, carry):
        d = jnp.dot(ki_ref[0, c], qcat_ref[...], preferred_element_type=F32)
        s = jnp.zeros((KC, QB), F32)
        for h in range(N_IDX_HEADS):
            s = s + w[h:h + 1, :] * jnp.maximum(d[:, h * QB:(h + 1) * QB], 0.0)
        key = jnp.where(c * KC + row <= qpos, _sortable(s), KEY_NEG_INF)
        key_ref[c] = key
        half_ref[c] = (key >> 16).astype(I16)
        return carry

    lax.fori_loop(0, nck, score_chunk, 0)

    def count(pred):
        def body(c, acc):
            hit = pred(c, key_ref[c])
            return acc + jnp.sum(hit.reshape(KC // SUBLANES, SUBLANES, QB), axis=0)
        acc = lax.fori_loop(0, nck, body, jnp.zeros((SUBLANES, QB), F32))
        return jnp.sum(acc, axis=0, keepdims=True)

    half_tile = 2 * SUBLANES

    def count_half_ge(cand):
        cand16 = cand.astype(I16)

        def body(c, accs):
            hit = jnp.where(half_ref[c] >= cand16, jnp.int16(1), jnp.int16(0))
            hit = hit.reshape(KC // half_tile, half_tile, QB)
            accs = list(accs)
            for r in range(KC // half_tile):
                accs[r % len(accs)] = accs[r % len(accs)] + hit[r]
            return tuple(accs)

        zero = jnp.zeros((half_tile, QB), I16)
        accs = lax.fori_loop(0, nck, body, (zero,) * 4)
        acc = (accs[0] + accs[1]) + (accs[2] + accs[3])
        return jnp.sum(acc.astype(I32), axis=0, keepdims=True)

    def kth_largest_half(rank):
        def step(it, t):
            cand = t + lax.shift_left(jnp.int32(1), 15 - it)
            return jnp.where(count_half_ge(cand) >= rank, cand, t)
        return lax.fori_loop(0, 16, step, jnp.full((1, QB), I16_MIN, I32))

    t_hi = kth_largest_half(kk)
    above = jnp.where(t_hi < I16_MAX, count_half_ge(jnp.minimum(t_hi + 1, I16_MAX)), 0)

    def low_halves(c, carry):
        key = key_ref[c]
        low = (key & 0xFFFF) + I16_MIN
        half_ref[c] = jnp.where((key >> 16) == t_hi, low, I16_MIN).astype(I16)
        return carry

    lax.fori_loop(0, nck, low_halves, 0)
    t_lo = kth_largest_half(kk - above)
    thr = t_hi * 65536 + (t_lo - I16_MIN)
    need = kk - count(lambda c, k: jnp.where(k > thr, 1.0, 0.0))
    n_eq = count(lambda c, k: jnp.where(k == thr, 1.0, 0.0))
    surplus = jnp.max(jnp.where(thr > KEY_NEG_INF, n_eq - need, 0.0))

    def break_ties():
        def index_step(it, j):
            cand = j + lax.shift_left(jnp.int32(1), idx_bits - 1 - it)
            cnt = count(lambda c, k: jnp.where(k == thr, jnp.where(c * KC + row < cand, 1.0, 0.0), 0.0))
            return jnp.where(cnt < need, cand, j)
        return lax.fori_loop(0, idx_bits, index_step, jnp.zeros((1, QB), I32))

    jlim = lax.cond(surplus > 0.0, break_ties, lambda: jnp.full((1, QB), NO_TIE_LIMIT, I32))

    def write_mask(c, carry):
        k = key_ref[c]
        tie = jnp.where(c * KC + row <= jlim, 0.0, MASK_NEG)
        sel = jnp.where(k > thr, 0.0, jnp.where(k == thr, tie, MASK_NEG))
        mask = jnp.where(k > KEY_NEG_INF, sel, MASK_NEG)
        mb_ref[pl.ds(c * (KC // KS), KC // KS)] = mask.reshape(KC // KS, KS, QB)
        return carry

    lax.fori_loop(0, nck, write_mask, 0)

    qt = qt_ref[0] * ((HEAD_DIM ** -0.5) * LOG2E)
    zero_blk = jnp.zeros((HEAD_DIM, QB), BF16)
    for g in range(N_HEADS // HEAD_GROUP):
        for j in range(HEAD_GROUP):
            h = g * HEAD_GROUP + j
            qh = qt[h * HEAD_DIM:(h + 1) * HEAD_DIM, :].astype(BF16)
            qbd_ref[g, :, j * QB:(j + 1) * QB] = jnp.concatenate(
                [qh if r == j else zero_blk for r in range(HEAD_GROUP)], axis=0)
    m_ref[...] = jnp.full(m_ref.shape, MASK_NEG, F32)
    l_ref[...] = jnp.zeros(l_ref.shape, F32)
    acc_ref[...] = jnp.zeros((D_ATT, QB), F32)
    tiles = lambda x: x.reshape(x.shape[0] // SUBLANES, SUBLANES, QB)
    gw = HEAD_GROUP * HEAD_DIM

    def attend(c, band_sel):
        mbc = mb_ref[c]
        for g in range(N_HEADS // HEAD_GROUP):
            sg = jnp.dot(k_ref[0, c, :, g * gw:(g + 1) * gw], qbd_ref[g], preferred_element_type=F32)
            for j in range(HEAD_GROUP):
                h = g * HEAD_GROUP + j
                hs = slice(h * HEAD_DIM, (h + 1) * HEAD_DIM)
                s = sg[:, j * QB:(j + 1) * QB] + mbc
                if band_sel is not None:
                    s = s + band_ref[band_sel, h]
                s = tiles(s)
                m_old = m_ref[h]
                m_new = jnp.maximum(m_old, jnp.max(jnp.max(s, axis=0), axis=0, keepdims=True))
                alpha = jnp.exp2(m_old - m_new)
                p = jnp.exp2(s - m_new[None])
                l_ref[h] = alpha * l_ref[h] + jnp.sum(jnp.sum(p, axis=0), axis=0, keepdims=True)
                m_ref[h] = m_new
                pv = jnp.dot(vt_ref[0, c, hs, :], p.reshape(KS, QB).astype(BF16), preferred_element_type=F32)
                acc_ref[hs, :] = (alpha[None] * tiles(acc_ref[hs, :]) + tiles(pv)).reshape(HEAD_DIM, QB)

    def far_step(c, carry):
        attend(c, None)
        return carry

    n_far = jnp.maximum((QB // KS) * i - 1, 0)

    def near_step(c, carry):
        attend(c, c - (QB // KS) * i + 1)
        return carry

    lax.fori_loop(0, n_far, far_step, 0)
    lax.fori_loop(n_far, (QB // KS) * nck, near_step, 0)
    for hp in range(N_HEADS // 2):
        pair = [(tiles(acc_ref[h * HEAD_DIM:(h + 1) * HEAD_DIM, :]) / l_ref[h][None]).reshape(HEAD_DIM, QB)
                for h in (2 * hp, 2 * hp + 1)]
        o_ref[0, :, hp * LANES:(hp + 1) * LANES] = jnp.concatenate(pair, axis=0).T


def _shifted_bias(rel_bias, buckets):
    far = rel_bias[N_BUCKETS - 1]
    lead = (N_HEADS,) + (1,) * buckets.ndim
    idx = jnp.asarray(buckets)[None]
    out = jnp.zeros((N_HEADS,) + buckets.shape, F32)
    for b in np.unique(buckets):
        if b != N_BUCKETS - 1:
            out = jnp.where(idx == b, (rel_bias[b] - far).reshape(lead), out)
    return out


def _prompt_bands(rel_bias):
    tab = _rel_bucket_table(QB + KS)
    sj = np.arange(KS)[:, None]
    qi = np.arange(QB)[None, :]
    buckets = np.stack([tab[np.maximum(qi - sj - (n - 1) * KS, 0)] for n in range(N_NEAR)])
    return jnp.transpose(_shifted_bias(rel_bias, buckets), (1, 0, 2, 3)) * LOG2E


def _prompt_attention(qt, k, vtc, qit, kw, kwt, bands):
    b, t = qt.shape[0], qt.shape[2]
    nc = t // KC
    ns = t // KS
    kk = min(TOPK_MAX, t // 4)
    kc = k.reshape(b, ns, KS, D_ATT)
    ki = kw.reshape(b, nc, KC, LANES)
    col = lambda n: pl.BlockSpec((1, n, QB), lambda bi, i: (bi, 0, i))
    res = lambda s: pl.BlockSpec((1,) + s, lambda bi, i: (bi, 0, 0, 0))
    w_rows = pl.BlockSpec((1, SUBLANES, QB), lambda bi, i: (bi, D_IDX // SUBLANES, i))
    kern = functools.partial(_prompt_attn_kernel, kk=kk, idx_bits=max(1, (t - 1).bit_length()))
    return pl.pallas_call(
        kern,
        grid=(b, t // QB),
        in_specs=[col(D_ATT), res((ns, KS, D_ATT)), res((ns, D_ATT, KS)),
                  col(N_IDX_HEADS * D_IDX), res((nc, KC, LANES)), w_rows,
                  pl.BlockSpec(bands.shape, lambda bi, i: (0, 0, 0, 0))],
        out_specs=pl.BlockSpec((1, QB, D_ATT), lambda bi, i: (bi, i, 0)),
        out_shape=jax.ShapeDtypeStruct((b, t, D_ATT), F32),
        scratch_shapes=[pltpu.VMEM((nc, KC, QB), I32), pltpu.VMEM((nc, KC, QB), I16),
                        pltpu.VMEM((ns, KS, QB), F32),
                        pltpu.VMEM((N_HEADS, SUBLANES, QB), F32), pltpu.VMEM((N_HEADS, SUBLANES, QB), F32),
                        pltpu.VMEM((N_HEADS // HEAD_GROUP, HEAD_GROUP * HEAD_DIM, HEAD_GROUP * QB), BF16),
                        pltpu.VMEM((D_ATT, QB), F32),
                        pltpu.VMEM((LANES, N_IDX_HEADS * QB), BF16)],
        compiler_params=_cparams(2),
        name="prompt_attention",
    )(qt, kc, vtc, qit, ki, kwt, bands)


def _s5_coeffs(lam_re, lam_im, log_step):
    lr = jnp.minimum(lam_re, -1e-4)
    li = lam_im
    dt = jnp.exp(log_step)
    mag = jnp.exp(lr * dt)
    a_re = mag * jnp.cos(li * dt)
    a_im = mag * jnp.sin(li * dt)
    den = lr * lr + li * li
    f_re = ((a_re - 1.0) * lr + a_im * li) / den
    f_im = (a_im * lr - (a_re - 1.0) * li) / den
    return a_re, a_im, f_re, f_im


def _s5_input_drive(u, f_re, f_im, bre_ref, bim_ref):
    w_re = f_re * bre_ref[...] - f_im * bim_ref[...]
    w_im = f_re * bim_ref[...] + f_im * bre_ref[...]
    ub = u.astype(BF16)
    return _bdot(ub, w_re), _bdot(ub, w_im)


def _s5_readout(h_re, h_im, u, gate, cre_ref, cim_ref, d_ref, gw_ref, gb_ref):
    y = _bdot(h_re, cre_ref[...]) - _bdot(h_im, cim_ref[...])
    y = y + d_ref[...] * u
    z = _gelu_tanh(y)
    z = z * _sigmoid(_bdot(z, gw_ref[...]) + gb_ref[...])
    return z * _silu(gate)


def _cmul(ar, ai, br, bi):
    return ar * br - ai * bi, ar * bi + ai * br


def _s5_seq_kernel(x_ref, h0re_ref, h0im_ref, lre_ref, lim_ref, ls_ref, bre_ref, bim_ref,
                   cre_ref, cim_ref, d_ref, gw_ref, gb_ref,
                   o_ref, hre_out, him_out, hre_s, him_s, cre_s, cim_s):
    c = pl.program_id(1)
    tc = x_ref.shape[1]

    @pl.when(c == 0)
    def _():
        cre_s[...] = h0re_ref[0]
        cim_s[...] = h0im_ref[0]

    u = x_ref[0, :, 0:D_SSM]
    gate = x_ref[0, :, D_SSM:]
    a_re, a_im, f_re, f_im = _s5_coeffs(lre_ref[...], lim_ref[...], ls_ref[...])
    bu_re, bu_im = _s5_input_drive(u, f_re, f_im, bre_ref, bim_ref)
    hre_s[...] = bu_re
    him_s[...] = bu_im

    full = lambda x: jnp.broadcast_to(x, (SUBLANES, S5_P))
    pows = [(full(a_re), full(a_im))]
    for _ in range(SUBLANES - 1):
        pows.append(_cmul(pows[-1][0], pows[-1][1], pows[0][0], pows[0][1]))
    row = lax.broadcasted_iota(I32, (SUBLANES, S5_P), 0)
    pw_re, pw_im = pows[SUBLANES - 1]
    for r in range(SUBLANES - 2, -1, -1):
        pw_re = jnp.where(row == r, pows[r][0], pw_re)
        pw_im = jnp.where(row == r, pows[r][1], pw_im)

    def tile(j, carry):
        cr, ci = carry
        sl = pl.ds(pl.multiple_of(j * SUBLANES, SUBLANES), SUBLANES)
        xr = hre_s[sl, :]
        xi = him_s[sl, :]
        for s in (1, 2, 4):
            sr = jnp.where(row >= s, pltpu.roll(xr, s, 0), 0.0)
            si = jnp.where(row >= s, pltpu.roll(xi, s, 0), 0.0)
            pr, pi = _cmul(pows[s - 1][0], pows[s - 1][1], sr, si)
            xr = xr + pr
            xi = xi + pi
        pr, pi = _cmul(pw_re, pw_im, cr, ci)
        xr = xr + pr
        xi = xi + pi
        hre_s[sl, :] = xr
        him_s[sl, :] = xi
        return xr[SUBLANES - 1:, :], xi[SUBLANES - 1:, :]

    cr, ci = lax.fori_loop(0, tc // SUBLANES, tile, (cre_s[...], cim_s[...]))
    cre_s[...] = cr
    cim_s[...] = ci
    hre_out[0] = cr
    him_out[0] = ci
    o_ref[0] = _s5_readout(hre_s[...], him_s[...], u, gate, cre_ref, cim_ref, d_ref, gw_ref, gb_ref)


def _s5_step_kernel(x_ref, h0re_ref, h0im_ref, lre_ref, lim_ref, ls_ref, bre_ref, bim_ref,
                    cre_ref, cim_ref, d_ref, gw_ref, gb_ref, o_ref, hre_out, him_out):
    u = x_ref[:, 0:D_SSM]
    gate = x_ref[:, D_SSM:]
    a_re, a_im, f_re, f_im = _s5_coeffs(lre_ref[...], lim_ref[...], ls_ref[...])
    bu_re, bu_im = _s5_input_drive(u, f_re, f_im, bre_ref, bim_ref)
    pr, pi = _cmul(a_re, a_im, h0re_ref[...], h0im_ref[...])
    h_re = bu_re + pr
    h_im = bu_im + pi
    hre_out[...] = h_re
    him_out[...] = h_im
    o_ref[...] = _s5_readout(h_re, h_im, u, gate, cre_ref, cim_ref, d_ref, gw_ref, gb_ref)


def _s5_params(lp):
    eye = jnp.eye(N_SSM_GROUPS, dtype=F32)

    def in_bd(b):
        return (jnp.transpose(b, (0, 2, 1))[:, :, None, :] * eye[:, None, :, None]).reshape(D_SSM, S5_P)

    def out_bd(c):
        return (jnp.transpose(c, (0, 2, 1))[:, :, None, :] * eye[:, None, :, None]).reshape(S5_P, D_SSM)

    ls = jnp.broadcast_to(lp['log_step'][:, None], (N_SSM_GROUPS, SSM_STATE))
    return (lp['lam_re'].reshape(1, S5_P), lp['lam_im'].reshape(1, S5_P), ls.reshape(1, S5_P),
            in_bd(lp['b_re']), in_bd(lp['b_im']),
            out_bd(lp['c_re']).astype(BF16), out_bd(lp['c_im']).astype(BF16),
            lp['d_skip'].reshape(1, D_SSM), lp['glu_w'].astype(BF16), lp['glu_b'].reshape(1, D_SSM))


def _s5_seq(x, h0_re, h0_im, params):
    b, t = x.shape[0], x.shape[1]
    tc = min(SCAN_CHUNK, t)
    full = lambda a: pl.BlockSpec(a.shape, lambda bi, c: (0,) * a.ndim)
    st = pl.BlockSpec((1, 1, S5_P), lambda bi, c: (bi, 0, 0))
    return pl.pallas_call(
        _s5_seq_kernel,
        grid=(b, t // tc),
        in_specs=[pl.BlockSpec((1, tc, 2 * D_SSM), lambda bi, c: (bi, c, 0)), st, st]
                 + [full(a) for a in params],
        out_specs=[pl.BlockSpec((1, tc, D_SSM), lambda bi, c: (bi, c, 0)), st, st],
        out_shape=[jax.ShapeDtypeStruct((b, t, D_SSM), F32),
                   jax.ShapeDtypeStruct((b, 1, S5_P), F32),
                   jax.ShapeDtypeStruct((b, 1, S5_P), F32)],
        scratch_shapes=[pltpu.VMEM((tc, S5_P), F32), pltpu.VMEM((tc, S5_P), F32),
                        pltpu.VMEM((1, S5_P), F32), pltpu.VMEM((1, S5_P), F32)],
        compiler_params=_cparams(2),
        name="s5_sequence",
    )(x, h0_re, h0_im, *params)


def _s5_step(x, h0_re, h0_im, params):
    b = x.shape[0]
    return pl.pallas_call(
        _s5_step_kernel,
        out_shape=[jax.ShapeDtypeStruct((b, D_SSM), F32),
                   jax.ShapeDtypeStruct((b, S5_P), F32),
                   jax.ShapeDtypeStruct((b, S5_P), F32)],
        compiler_params=pltpu.CompilerParams(vmem_limit_bytes=VMEM_LIMIT),
        name="s5_step",
    )(x, h0_re, h0_im, *params)


def _lru_gates(xc, wa_ref, ba_ref, wx_ref, bx_ref, lam_ref):
    xb = xc.astype(BF16)
    r = _sigmoid(_bdot(xb, wa_ref[...]) + ba_ref[...])
    g = _sigmoid(_bdot(xb, wx_ref[...]) + bx_ref[...])
    log_a = -LRU_C * r * _softplus(-lam_ref[...])
    a = jnp.exp(log_a)
    b = jnp.sqrt(-jnp.tanh(log_a) * (a * a + 1.0)) * (g * xc)
    return a, b


def _lru_seq_kernel(x_ref, conv0_ref, h0_ref, cw_ref, cb_ref, wa_ref, ba_ref, wx_ref, bx_ref, lam_ref,
                    o_ref, h_out, conv_out, ubuf, a_s, b_s, c_s):
    c = pl.program_id(1)
    tc = x_ref.shape[1]
    hist = CONV_W - 1

    @pl.when(c == 0)
    def _():
        ubuf[0:SUBLANES, :] = conv0_ref[0]
        c_s[...] = h0_ref[0]

    u = x_ref[0, :, 0:D_LRU]
    gate = x_ref[0, :, D_LRU:]
    ubuf[SUBLANES:, :] = u
    xc = cb_ref[...] + u * cw_ref[hist:hist + 1, :]
    for j in range(hist):
        xc = xc + ubuf[SUBLANES - hist + j:SUBLANES - hist + j + tc, :] * cw_ref[j:j + 1, :]
    tail = ubuf[tc:tc + SUBLANES, :]
    ubuf[0:SUBLANES, :] = tail
    conv_out[0] = tail

    a, b = _lru_gates(xc, wa_ref, ba_ref, wx_ref, bx_ref, lam_ref)
    a_s[...] = a
    b_s[...] = b
    row = lax.broadcasted_iota(I32, (SUBLANES, D_LRU), 0)

    def tile(j, carry):
        sl = pl.ds(pl.multiple_of(j * SUBLANES, SUBLANES), SUBLANES)
        av = a_s[sl, :]
        bv = b_s[sl, :]
        for s in (1, 2, 4):
            a_sh = jnp.where(row >= s, pltpu.roll(av, s, 0), 1.0)
            b_sh = jnp.where(row >= s, pltpu.roll(bv, s, 0), 0.0)
            bv = av * b_sh + bv
            av = av * a_sh
        h = bv + av * carry
        b_s[sl, :] = h
        return h[SUBLANES - 1:, :]

    carry = lax.fori_loop(0, tc // SUBLANES, tile, c_s[...])
    c_s[...] = carry
    h_out[0] = carry
    o_ref[0] = b_s[...] * _silu(gate)


def _lru_step_kernel(x_ref, c0_ref, c1_ref, c2_ref, h0_ref, cw_ref, cb_ref, wa_ref, ba_ref, wx_ref,
                     bx_ref, lam_ref, o_ref, h_out):
    u = x_ref[:, 0:D_LRU]
    gate = x_ref[:, D_LRU:]
    xc = (cb_ref[...] + c0_ref[...] * cw_ref[0:1, :] + c1_ref[...] * cw_ref[1:2, :]
          + c2_ref[...] * cw_ref[2:3, :] + u * cw_ref[3:4, :])
    a, b = _lru_gates(xc, wa_ref, ba_ref, wx_ref, bx_ref, lam_ref)
    h = a * h0_ref[...] + b
    h_out[...] = h
    o_ref[...] = h * _silu(gate)


def _lru_params(lp):
    eye = jnp.eye(N_LRU_BLOCKS, dtype=F32)
    bd = lambda w: (w[:, :, None, :] * eye[:, None, :, None]).reshape(D_LRU, D_LRU).astype(BF16)
    cw = jnp.concatenate([lp['conv_w'], jnp.zeros((SUBLANES - CONV_W, D_LRU), F32)], axis=0)
    return (cw, lp['conv_b'].reshape(1, D_LRU), bd(lp['wa']), lp['ba'].reshape(1, D_LRU),
            bd(lp['wx']), lp['bx'].reshape(1, D_LRU), lp['lam'].reshape(1, D_LRU))


def _lru_seq(x, conv0, h0, params):
    b, t = x.shape[0], x.shape[1]
    tc = min(SCAN_CHUNK, t)
    full = lambda a: pl.BlockSpec(a.shape, lambda bi, c: (0,) * a.ndim)
    return pl.pallas_call(
        _lru_seq_kernel,
        grid=(b, t // tc),
        in_specs=[pl.BlockSpec((1, tc, 2 * D_LRU), lambda bi, c: (bi, c, 0)),
                  pl.BlockSpec((1, SUBLANES, D_LRU), lambda bi, c: (bi, 0, 0)),
                  pl.BlockSpec((1, 1, D_LRU), lambda bi, c: (bi, 0, 0))]
                 + [full(a) for a in params],
        out_specs=[pl.BlockSpec((1, tc, D_LRU), lambda bi, c: (bi, c, 0)),
                   pl.BlockSpec((1, 1, D_LRU), lambda bi, c: (bi, 0, 0)),
                   pl.BlockSpec((1, SUBLANES, D_LRU), lambda bi, c: (bi, 0, 0))],
        out_shape=[jax.ShapeDtypeStruct((b, t, D_LRU), F32),
                   jax.ShapeDtypeStruct((b, 1, D_LRU), F32),
                   jax.ShapeDtypeStruct((b, SUBLANES, D_LRU), F32)],
        scratch_shapes=[pltpu.VMEM((tc + SUBLANES, D_LRU), F32), pltpu.VMEM((tc, D_LRU), F32),
                        pltpu.VMEM((tc, D_LRU), F32), pltpu.VMEM((1, D_LRU), F32)],
        compiler_params=_cparams(2),
        name="lru_sequence",
    )(x, conv0, h0, *params)


def _lru_step(x, conv0, h0, params):
    b = x.shape[0]
    return pl.pallas_call(
        _lru_step_kernel,
        out_shape=[jax.ShapeDtypeStruct((b, D_LRU), F32), jax.ShapeDtypeStruct((b, D_LRU), F32)],
        compiler_params=pltpu.CompilerParams(vmem_limit_bytes=VMEM_LIMIT),
        name="lru_step",
    )(x, conv0[:, 0], conv0[:, 1], conv0[:, 2], h0, *params)


def _dec_score_kernel(pt_ref, qi_ref, w_ref, *refs):
    pages, o_ref = refs[:-1], refs[-1]
    qi = qi_ref[0].astype(BF16)
    w = w_ref[0] * (N_IDX_HEADS ** -0.5)
    for g, page in enumerate(pages):
        d = jnp.dot(qi, page[0, 0].astype(BF16), preferred_element_type=F32) * (D_IDX ** -0.5)
        o_ref[0, 0, g:g + 1, :] = jnp.sum(w * jnp.maximum(d, 0.0), axis=0, keepdims=True)


def _dec_scores(q_idx, w_idx, kidx_t, page_table, layer):
    b = q_idx.shape[0]
    n_pages = page_table.shape[1]
    page = kidx_t.shape[3]
    g = math.gcd(SCORE_PAGES_PER_STEP, n_pages)
    page_spec = lambda j: pl.BlockSpec(
        (1, 1, D_IDX, page), lambda bi, p, pt: (layer, pt[bi * n_pages + p * g + j], 0, 0))
    out = pl.pallas_call(
        _dec_score_kernel,
        grid_spec=pltpu.PrefetchScalarGridSpec(
            num_scalar_prefetch=1,
            grid=(b, n_pages // g),
            in_specs=[pl.BlockSpec((1, N_IDX_HEADS, D_IDX), lambda bi, p, pt: (bi, 0, 0)),
                      pl.BlockSpec((1, N_IDX_HEADS, 1), lambda bi, p, pt: (bi, 0, 0))]
                     + [page_spec(j) for j in range(g)],
            out_specs=pl.BlockSpec((1, 1, g, page), lambda bi, p, pt: (bi, p, 0, 0)),
        ),
        out_shape=jax.ShapeDtypeStruct((b, n_pages // g, g, page), F32),
        compiler_params=_cparams(2),
        name="decode_scores",
    )(page_table.reshape(-1), q_idx.reshape(b, N_IDX_HEADS, D_IDX),
      w_idx.reshape(b, N_IDX_HEADS, 1), *([kidx_t] * g))
    return out.reshape(b, n_pages * page)


def _dec_select_kernel(s_ref, qi_ref, kn_ref, w_ref, mb_ref, mbn_ref, *, kk, idx_bits):
    b, past = s_ref.shape
    n = past + LANES
    qi = qi_ref[...]
    kn = kn_ref[...]
    w = w_ref[...] * (N_IDX_HEADS ** -0.5)
    s_new = jnp.zeros((b, 1), F32)
    for h in range(N_IDX_HEADS):
        d = jnp.sum(qi[:, h * D_IDX:(h + 1) * D_IDX] * kn, axis=1, keepdims=True) * (D_IDX ** -0.5)
        s_new = s_new + w[:, h:h + 1] * jnp.maximum(d, 0.0)
    lane = lax.broadcasted_iota(I32, (1, LANES), 1)
    tail = jnp.where(lane == 0, _sortable(jnp.broadcast_to(s_new, (b, LANES))), KEY_NEG_INF)
    keys = jnp.concatenate([_sortable(s_ref[...]), tail], axis=1)
    pos = lax.broadcasted_iota(I32, (1, n), 1)

    def count(m):
        return jnp.sum(m, axis=1, keepdims=True)

    def value_step(it, t):
        cand = t + lax.shift_left(jnp.int32(1), 31 - it)
        return jnp.where(count(jnp.where(keys >= cand, 1.0, 0.0)) >= kk, cand, t)

    thr = lax.fori_loop(0, 32, value_step, jnp.full((b, 1), INT_MIN, I32))
    need = kk - count(jnp.where(keys > thr, 1.0, 0.0))

    def index_step(it, j):
        cand = j + lax.shift_left(jnp.int32(1), idx_bits - 1 - it)
        c = count(jnp.where(keys == thr, jnp.where(pos < cand, 1.0, 0.0), 0.0))
        return jnp.where(c < need, cand, j)

    jlim = lax.fori_loop(0, idx_bits, index_step, jnp.zeros((b, 1), I32))
    tie = jnp.where(pos <= jlim, 0.0, MASK_NEG)
    sel = jnp.where(keys > thr, 0.0, jnp.where(keys == thr, tie, MASK_NEG))
    sel = jnp.where(keys > KEY_NEG_INF, sel, MASK_NEG)
    mb_ref[...] = sel[:, :past]
    mbn_ref[...] = sel[:, past:]


def _dec_select(scores, q_idx, k_idx_new, w_idx, kk):
    b, past = scores.shape
    kern = functools.partial(_dec_select_kernel, kk=kk, idx_bits=max(1, (past + LANES - 1).bit_length()))
    return pl.pallas_call(
        kern,
        out_shape=[jax.ShapeDtypeStruct((b, past), F32), jax.ShapeDtypeStruct((b, LANES), F32)],
        compiler_params=pltpu.CompilerParams(vmem_limit_bytes=VMEM_LIMIT),
        name="decode_select",
    )(scores, q_idx, k_idx_new, w_idx)


def _dec_attn_kernel(pt_ref, qb_ref, q_ref, kn_ref, vn_ref, mb_ref, mbn_ref, bias_ref, biasn_ref, *refs):
    g = (len(refs) - 4) // 2
    k_refs, v_refs = refs[:g], refs[g:2 * g]
    o_ref, m_s, l_s, acc_s = refs[2 * g:]
    step = pl.program_id(1)
    scale = HEAD_DIM ** -0.5

    @pl.when(step == 0)
    def _():
        m_s[...] = jnp.full(m_s.shape, MASK_NEG, F32)
        l_s[...] = jnp.zeros(l_s.shape, F32)
        acc_s[...] = jnp.zeros(acc_s.shape, F32)

    qb = qb_ref[0]
    s = jnp.concatenate([jnp.sum(kr[0, 0] * qb, axis=1) for kr in k_refs], axis=1)
    s = s * scale + bias_ref[...] + mb_ref[0]
    m_old = m_s[...]
    m_new = jnp.maximum(m_old, jnp.max(s, axis=1, keepdims=True))
    alpha = jnp.exp(m_old - m_new)
    p = jnp.exp(s - m_new)
    l_s[...] = alpha * l_s[...] + jnp.sum(p, axis=1, keepdims=True)
    m_s[...] = m_new
    page = qb.shape[2]
    for h in range(N_HEADS):
        upd = acc_s[h] * alpha[h:h + 1, :]
        for j, vr in enumerate(v_refs):
            upd = upd + p[h:h + 1, j * page:(j + 1) * page] * vr[0, 0, h]
        acc_s[h] = upd

    @pl.when(step == pl.num_programs(1) - 1)
    def _():
        ones = jnp.ones((1, page), F32)
        ctx = jnp.concatenate(
            [lax.dot_general(ones, acc_s[h], (((1,), (1,)), ((), ())), precision=HI,
                             preferred_element_type=F32) for h in range(N_HEADS)], axis=0)
        s_new = (jnp.sum(q_ref[0] * kn_ref[0], axis=1, keepdims=True) * scale
                 + biasn_ref[...] + mbn_ref[0][:, 0:1])
        m_f = jnp.maximum(m_s[...], s_new)
        a = jnp.exp(m_s[...] - m_f)
        pn = jnp.exp(s_new - m_f)
        o_ref[0] = (a * ctx + pn * vn_ref[0]) / (a * l_s[...] + pn)


def _dec_attention(q, k_new, v_new, mb, mb_new, bias, bias_new, ck_t, cv_t, page_table, layer):
    b = q.shape[0]
    n_pages = page_table.shape[1]
    page = ck_t.shape[4]
    g = math.gcd(ATTN_PAGES_PER_STEP, n_pages)
    steps = n_pages // g
    q3 = q.reshape(b, N_HEADS, HEAD_DIM)
    qb = jnp.broadcast_to(q3[:, :, :, None], (b, N_HEADS, HEAD_DIM, page))
    row = lambda: pl.BlockSpec((1, N_HEADS, HEAD_DIM), lambda bi, p, pt: (bi, 0, 0))
    page_spec = lambda j: pl.BlockSpec(
        (1, 1, N_HEADS, HEAD_DIM, page), lambda bi, p, pt: (layer, pt[bi * n_pages + p * g + j], 0, 0, 0))
    return pl.pallas_call(
        _dec_attn_kernel,
        grid_spec=pltpu.PrefetchScalarGridSpec(
            num_scalar_prefetch=1,
            grid=(b, steps),
            in_specs=[pl.BlockSpec((1, N_HEADS, HEAD_DIM, page), lambda bi, p, pt: (bi, 0, 0, 0)),
                      row(), row(), row(),
                      pl.BlockSpec((1, 1, g * page), lambda bi, p, pt: (bi, 0, p)),
                      pl.BlockSpec((1, 1, LANES), lambda bi, p, pt: (bi, 0, 0)),
                      pl.BlockSpec((N_HEADS, g * page), lambda bi, p, pt: (0, p)),
                      pl.BlockSpec((N_HEADS, 1), lambda bi, p, pt: (0, 0))]
                     + [page_spec(j) for j in range(g)] * 2,
            out_specs=row(),
            scratch_shapes=[pltpu.VMEM((N_HEADS, 1), F32), pltpu.VMEM((N_HEADS, 1), F32),
                            pltpu.VMEM((N_HEADS, HEAD_DIM, page), F32)],
        ),
        out_shape=jax.ShapeDtypeStruct((b, N_HEADS, HEAD_DIM), F32),
        compiler_params=_cparams(2),
        name="decode_attention",
    )(page_table.reshape(-1), qb, q3, k_new.reshape(b, N_HEADS, HEAD_DIM),
      v_new.reshape(b, N_HEADS, HEAD_DIM), mb.reshape(b, 1, n_pages * page),
      mb_new.reshape(b, 1, LANES), bias, bias_new, *([ck_t] * g), *([cv_t] * g))


def _decode_bias(rel_bias, past):
    tab = _rel_bucket_table(past + 1)
    bias = _shifted_bias(rel_bias, tab[past - np.arange(past)])
    return bias, (rel_bias[0] - rel_bias[N_BUCKETS - 1]).reshape(N_HEADS, 1)


def kernel(x_prompt, x_sample, cache_k, cache_v, cache_kidx, state_s5_re, state_s5_im, state_lru_h,
           state_lru_conv, page_table, w_in, w_out, ln_g, ln_b, rel_bias, s5_lam_re, s5_lam_im,
           s5_log_step, s5_b_re, s5_b_im, s5_c_re, s5_c_im, s5_d, glu_w, glu_b, lru_conv_w, lru_conv_b,
           lru_wa, lru_ba, lru_wx, lru_bx, lru_lam):
    depth = w_in.shape[0]
    bp, t = x_prompt.shape[0], x_prompt.shape[1]
    bs = x_sample.shape[0]
    n_pages = page_table.shape[1]
    page = cache_k.shape[2]
    past = n_pages * page
    kk_s = min(TOPK_MAX, (past + 1) // 4)
    bands = _prompt_bands(rel_bias)
    dec_bias, dec_bias_new = _decode_bias(rel_bias, past)
    hist = CONV_W - 1
    ck_t = jnp.transpose(cache_k, (0, 1, 3, 4, 2))
    cv_t = jnp.transpose(cache_v, (0, 1, 3, 4, 2))
    kidx_t = jnp.transpose(cache_kidx, (0, 1, 3, 2))

    xp = x_prompt.reshape(bp * t, D_MODEL)
    xs = x_sample.reshape(bs, D_MODEL)
    zero_state = jnp.zeros((bp, 1, S5_P), F32)
    zero_h = jnp.zeros((bp, 1, D_LRU), F32)
    zero_conv = jnp.zeros((bp, SUBLANES, D_LRU), F32)
    new_p = [[] for _ in range(7)]
    new_s = [[] for _ in range(7)]

    for l in range(depth):
        w_packed = _pack_w_in(w_in[l])
        w_out_l = w_out[l].astype(BF16)
        s5p = _s5_params({'lam_re': s5_lam_re[l], 'lam_im': s5_lam_im[l], 'log_step': s5_log_step[l],
                          'b_re': s5_b_re[l], 'b_im': s5_b_im[l], 'c_re': s5_c_re[l], 'c_im': s5_c_im[l],
                          'd_skip': s5_d[l], 'glu_w': glu_w[l], 'glu_b': glu_b[l]})
        lrup = _lru_params({'conv_w': lru_conv_w[l], 'conv_b': lru_conv_b[l], 'wa': lru_wa[l],
                            'ba': lru_ba[l], 'wx': lru_wx[l], 'bx': lru_bx[l], 'lam': lru_lam[l]})
        lng = ln_g[l].reshape(1, D_MODEL)
        lnb = ln_b[l].reshape(1, D_MODEL)

        hp = _in_proj_prompt(xp.reshape(bp, t, D_MODEL), w_packed)
        o_att = _prompt_attention(hp['qT'], hp['k'], hp['vtc'], hp['qiT'], hp['kw'], hp['kwT'], bands)
        o_att = o_att.reshape(bp * t, D_ATT)
        o_ssm, s5re, s5im = _s5_seq(hp['ssm'], zero_state, zero_state, s5p)
        o_lru, lruh, conv = _lru_seq(hp['lru'], zero_conv, zero_h, lrup)
        flat = lambda a: a.reshape(bp * t, a.shape[-1])
        xp_new = _out_ln(xp, o_att, flat(hp['g']), flat(o_ssm), flat(o_lru), w_out_l, lng, lnb,
                         "out_ln_prompt")
        for lst, a in zip(new_p, (hp['kT'].reshape(bp, N_HEADS, HEAD_DIM, t),
                                  hp['vT'].reshape(bp, N_HEADS, HEAD_DIM, t),
                                  hp['kwT'][:, :D_IDX, :],
                                  s5re.reshape(bp, N_SSM_GROUPS, SSM_STATE),
                                  s5im.reshape(bp, N_SSM_GROUPS, SSM_STATE),
                                  lruh.reshape(bp, D_LRU), conv[:, SUBLANES - hist:, :])):
            lst.append(a)
        xp = xp_new

        hs = _in_proj(xs, w_packed, "in_proj_sample")
        kidx_s = hs['kw'][:, :D_IDX]
        widx_s = hs['kw'][:, D_IDX:D_IDX + N_IDX_HEADS]
        scores = _dec_scores(hs['qi'], widx_s, kidx_t, page_table, l)
        mb, mb_new = _dec_select(scores, hs['qi'], kidx_s, widx_s, kk_s)
        o_att_s = _dec_attention(hs['q'], hs['k'], hs['v'], mb, mb_new, dec_bias, dec_bias_new,
                                 ck_t, cv_t, page_table, l)
        o_ssm_s, s5re_s, s5im_s = _s5_step(hs['ssm'], state_s5_re[l].reshape(bs, S5_P),
                                           state_s5_im[l].reshape(bs, S5_P), s5p)
        conv0 = state_lru_conv[l]
        o_lru_s, lruh_s = _lru_step(hs['lru'], conv0, state_lru_h[l], lrup)
        conv_s = jnp.concatenate([conv0[:, 1:], hs['lru'][:, None, :D_LRU]], axis=1)
        xs_new = _out_ln(xs, o_att_s.reshape(bs, D_ATT), hs['g'], o_ssm_s, o_lru_s, w_out_l, lng, lnb,
                         "out_ln_sample")
        for lst, a in zip(new_s, (hs['k'].reshape(bs, 1, N_HEADS, HEAD_DIM),
                                  hs['v'].reshape(bs, 1, N_HEADS, HEAD_DIM),
                                  kidx_s.reshape(bs, 1, D_IDX),
                                  s5re_s.reshape(bs, N_SSM_GROUPS, SSM_STATE),
                                  s5im_s.reshape(bs, N_SSM_GROUPS, SSM_STATE),
                                  lruh_s, conv_s)):
            lst.append(a)
        xs = xs_new

    outs_p = [jnp.stack(a) for a in new_p]
    outs_p[0] = jnp.transpose(outs_p[0], (0, 1, 4, 2, 3))
    outs_p[1] = jnp.transpose(outs_p[1], (0, 1, 4, 2, 3))
    outs_p[2] = jnp.transpose(outs_p[2], (0, 1, 3, 2))
    outs_s = [jnp.stack(a) for a in new_s]
    return (xp.reshape(bp, t, D_MODEL), xs.reshape(bs, 1, D_MODEL), *outs_p, *outs_s)
```

```python
import functools
import math

import numpy as np
import jax
import jax.numpy as jnp
from jax import lax
from jax.experimental import pallas as pl
from jax.experimental.pallas import tpu as pltpu

F32 = jnp.float32
BF16 = jnp.bfloat16
I32 = jnp.int32
I16 = jnp.int16
I16_MIN = -2 ** 15
I16_MAX = 2 ** 15 - 1
HI = lax.Precision.HIGHEST

D_MODEL = 1024
D_ATT = 512
HEAD_DIM = 64
N_HEADS = 8
N_IDX_HEADS = 4
D_IDX = 64
TOPK_MAX = 256
N_BUCKETS = 32
MAX_DISTANCE = 128
D_SSM = 256
SSM_CH = 16
N_SSM_GROUPS = 16
SSM_STATE = 64
S5_P = N_SSM_GROUPS * SSM_STATE
D_LRU = 256
N_LRU_BLOCKS = 4
CONV_W = 4
LRU_C = 8.0
LN_EPS = 1e-5
DEPTH = 4
DEEPNORM_ALPHA = (2.0 * DEPTH) ** 0.25
LOG2E = math.log2(math.e)

SUBLANES = 8
LANES = 128
VMEM_LIMIT = 56 * 1024 * 1024

QB = 256
KC = 256
KS = 256
N_NEAR = QB // KS + 1
HEAD_GROUP = 4
SCAN_CHUNK = 512
SCORE_PAGES_PER_STEP = 32
ATTN_PAGES_PER_STEP = 16
MASK_NEG = -1e30
INT_MIN = -2 ** 31
KEY_NEG_INF = -2139095041
NO_TIE_LIMIT = 2 ** 30


def _cparams(n_axes, flags=None):
    return pltpu.CompilerParams(dimension_semantics=("arbitrary",) * n_axes,
                                vmem_limit_bytes=VMEM_LIMIT, flags=flags)


def _bdot(a, b):
    return jnp.dot(a.astype(BF16), b.astype(BF16), preferred_element_type=F32)


def _sigmoid(x):
    return 1.0 / (1.0 + jnp.exp(-x))


def _silu(x):
    return x * _sigmoid(x)


def _gelu_tanh(x):
    return 0.5 * x * (1.0 + jnp.tanh(math.sqrt(2.0 / math.pi) * (x + 0.044715 * (x * x * x))))


def _softplus(x):
    return jnp.maximum(x, 0.0) + jnp.log1p(jnp.exp(-jnp.abs(x)))


def _sortable(x):
    bits = pltpu.bitcast(x, I32)
    return bits ^ ((bits >> 31) & 0x7FFFFFFF)


def _rel_bucket_table(n):
    d = np.arange(n)
    exact = N_BUCKETS // 2
    far = exact + (np.log(np.maximum(d, exact).astype(np.float32) / exact)
                   / math.log(MAX_DISTANCE / exact) * (N_BUCKETS - exact)).astype(np.int32)
    return np.where(d < exact, d, np.minimum(far, N_BUCKETS - 1)).astype(np.int32)


IN_WIDTHS = dict(q=D_ATT, k=D_ATT, v=D_ATT, g=D_ATT, qi=N_IDX_HEADS * D_IDX, kw=LANES,
                 ssm=2 * D_SSM, lru=2 * D_LRU)


_offsets = np.concatenate([[0], np.cumsum(list(IN_WIDTHS.values()))]).tolist()
IN_ROWS = {name: (_offsets[i], _offsets[i + 1]) for i, name in enumerate(IN_WIDTHS)}
CONTRACT_LAST = (((1,), (1,)), ((), ()))


def _rows_major(x, wt_ref, name):
    lo, hi = IN_ROWS[name]
    return lax.dot_general(x, wt_ref[lo:hi, :], CONTRACT_LAST, preferred_element_type=F32)


def _feature_major(x, wt_ref, name):
    lo, hi = IN_ROWS[name]
    return lax.dot_general(wt_ref[lo:hi, :], x, CONTRACT_LAST, preferred_element_type=F32)


def _in_proj_kernel(x_ref, wt_ref, *o_refs):
    x = x_ref[...].astype(BF16)
    for name, o_ref in zip(IN_WIDTHS, o_refs):
        o_ref[...] = _rows_major(x, wt_ref, name)


def _pack_w_in_t(w_in_t):
    widths = (D_ATT, D_ATT, D_ATT, D_ATT, N_IDX_HEADS * D_IDX, D_IDX, N_IDX_HEADS,
              D_SSM, D_SSM, D_LRU, D_LRU)
    cut = int(np.sum(widths[:7]))
    pad = jnp.zeros((LANES - D_IDX - N_IDX_HEADS, D_MODEL), w_in_t.dtype)
    return jnp.concatenate([w_in_t[:cut], pad, w_in_t[cut:]], axis=0).astype(BF16)


def _in_proj(x2d, wt_packed, name):
    m = x2d.shape[0]
    tm = min(512, m)
    outs = pl.pallas_call(
        _in_proj_kernel,
        grid=(m // tm,),
        in_specs=[pl.BlockSpec((tm, D_MODEL), lambda i: (i, 0)),
                  pl.BlockSpec(wt_packed.shape, lambda i: (0, 0))],
        out_specs=[pl.BlockSpec((tm, n), lambda i: (i, 0)) for n in IN_WIDTHS.values()],
        out_shape=[jax.ShapeDtypeStruct((m, n), F32) for n in IN_WIDTHS.values()],
        compiler_params=_cparams(1),
        name=name,
    )(x2d, wt_packed)
    return dict(zip(IN_WIDTHS.keys(), outs))


IN_T_GROUPS = ('q', 'k', 'v', 'qi', 'kw')
IN_N_GROUPS = ('k', 'g', 'kw', 'ssm', 'lru')
IN_N_DTYPES = dict(k=BF16, g=F32, kw=BF16, ssm=F32, lru=F32)


def _in_proj_prompt_kernel(x_ref, wt_ref, *o_refs):
    nt_refs = o_refs[:len(IN_T_GROUPS)]
    vtc_ref = o_refs[len(IN_T_GROUPS)]
    nn_refs = o_refs[len(IN_T_GROUPS) + 1:]
    x = x_ref[0].astype(BF16)
    for name, o_ref in zip(IN_T_GROUPS, nt_refs):
        res = _feature_major(x, wt_ref, name)
        o_ref[0] = res
        if name == 'v':
            for j in range(vtc_ref.shape[1]):
                vtc_ref[0, j] = res[:, j * KS:(j + 1) * KS].astype(BF16)
    for name, o_ref in zip(IN_N_GROUPS, nn_refs):
        o_ref[0] = _rows_major(x, wt_ref, name).astype(o_ref.dtype)


def _in_proj_prompt(x, wt_packed):
    b, t = x.shape[0], x.shape[1]
    tm = min(512, t)
    t_spec = lambda n: pl.BlockSpec((1, n, tm), lambda bi, j: (bi, 0, j))
    n_spec = lambda n: pl.BlockSpec((1, tm, n), lambda bi, j: (bi, j, 0))
    outs = pl.pallas_call(
        _in_proj_prompt_kernel,
        grid=(b, t // tm),
        in_specs=[n_spec(D_MODEL),
                  pl.BlockSpec(wt_packed.shape, lambda bi, j: (0, 0))],
        out_specs=[t_spec(IN_WIDTHS[n]) for n in IN_T_GROUPS]
                  + [pl.BlockSpec((1, tm // KS, D_ATT, KS), lambda bi, j: (bi, j, 0, 0))]
                  + [n_spec(IN_WIDTHS[n]) for n in IN_N_GROUPS],
        out_shape=[jax.ShapeDtypeStruct((b, IN_WIDTHS[n], t), F32) for n in IN_T_GROUPS]
                  + [jax.ShapeDtypeStruct((b, t // KS, D_ATT, KS), BF16)]
                  + [jax.ShapeDtypeStruct((b, t, IN_WIDTHS[n]), IN_N_DTYPES[n]) for n in IN_N_GROUPS],
        compiler_params=_cparams(2),
        name="in_proj_prompt",
    )(x, wt_packed)
    res = {n + 'T': o for n, o in zip(IN_T_GROUPS, outs)}
    res['vtc'] = outs[len(IN_T_GROUPS)]
    res.update(zip(IN_N_GROUPS, outs[len(IN_T_GROUPS) + 1:]))
    return res


def _out_ln_kernel(x_ref, att_ref, gatt_ref, ssm_ref, lru_ref, w_ref, g_ref, b_ref, o_ref):
    att = att_ref[...] * _silu(gatt_ref[...])
    out = _bdot(att, w_ref[0:D_ATT, :])
    out += _bdot(ssm_ref[...], w_ref[D_ATT:D_ATT + D_SSM, :])
    out += _bdot(lru_ref[...], w_ref[D_ATT + D_SSM:, :])
    y = DEEPNORM_ALPHA * x_ref[...] + out
    mu = jnp.mean(y, axis=-1, keepdims=True)
    yc = y - mu
    var = jnp.mean(yc * yc, axis=-1, keepdims=True)
    o_ref[...] = yc * lax.rsqrt(var + LN_EPS) * g_ref[...] + b_ref[...]


def _out_ln(x, att, gatt, ssm, lru, w_out, ln_g, ln_b, name):
    m = x.shape[0]
    tm = min(512, m)
    row = lambda n: pl.BlockSpec((tm, n), lambda i: (i, 0))
    full = lambda a: pl.BlockSpec(a.shape, lambda i: (0,) * a.ndim)
    return pl.pallas_call(
        _out_ln_kernel,
        grid=(m // tm,),
        in_specs=[row(D_MODEL), row(D_ATT), row(D_ATT), row(D_SSM), row(D_LRU),
                  full(w_out), full(ln_g), full(ln_b)],
        out_specs=row(D_MODEL),
        out_shape=jax.ShapeDtypeStruct((m, D_MODEL), F32),
        compiler_params=_cparams(1),
        name=name,
    )(x, att, gatt, ssm, lru, w_out, ln_g, ln_b)


def _prompt_attn_kernel(qt_ref, k_ref, vt_ref, qit_ref, ki_ref, wt_ref, band_ref, o_ref,
                        key_ref, half_ref, mb_ref, m_ref, l_ref, qbd_ref, acc_ref, qcat_ref, *, kk, idx_bits):
    i = pl.program_id(1)
    nck = i + 1
    qit = qit_ref[0].astype(BF16)
    zero_rows = jnp.zeros((LANES - D_IDX, QB), BF16)
    for h in range(N_IDX_HEADS):
        qcat_ref[:, h * QB:(h + 1) * QB] = jnp.concatenate(
            [qit[h * D_IDX:(h + 1) * D_IDX, :], zero_rows], axis=0)
    w = wt_ref[0] * (N_IDX_HEADS ** -0.5 * D_IDX ** -0.5)
    qpos = i * QB + lax.broadcasted_iota(I32, (1, QB), 1)
    row = lax.broadcasted_iota(I32, (KC, 1), 0)

    def score_chunk(c, carry):
        d = jnp.dot(ki_ref[0, c], qcat_ref[...], preferred_element_type=F32)
        s = jnp.zeros((KC, QB), F32)
        for h in range(N_IDX_HEADS):
            s = s + w[h:h + 1, :] * jnp.maximum(d[:, h * QB:(h + 1) * QB], 0.0)
        key = jnp.where(c * KC + row <= qpos, _sortable(s), KEY_NEG_INF)
        key_ref[c] = key
        half_ref[c] = (key >> 16).astype(I16)
        return carry

    lax.fori_loop(0, nck, score_chunk, 0)

    def count(pred):
        def body(c, acc):
            hit = pred(c, key_ref[c])
            return acc + jnp.sum(hit.reshape(KC // SUBLANES, SUBLANES, QB), axis=0)
        acc = lax.fori_loop(0, nck, body, jnp.zeros((SUBLANES, QB), F32))
        return jnp.sum(acc, axis=0, keepdims=True)

    half_tile = 2 * SUBLANES

    def count_half_ge(cand):
        cand16 = cand.astype(I16)

        def body(c, accs):
            hit = jnp.where(half_ref[c] >= cand16, jnp.int16(1), jnp.int16(0))
            hit = hit.reshape(KC // half_tile, half_tile, QB)
            accs = list(accs)
            for r in range(KC // half_tile):
                accs[r % len(accs)] = accs[r % len(accs)] + hit[r]
            return tuple(accs)

        zero = jnp.zeros((half_tile, QB), I16)
        accs = lax.fori_loop(0, nck, body, (zero,) * 4)
        acc = (accs[0] + accs[1]) + (accs[2] + accs[3])
        return jnp.sum(acc.astype(I32), axis=0, keepdims=True)

    def kth_largest_half(rank):
        def step(it, t):
            cand = t + lax.shift_left(jnp.int32(1), 15 - it)
            return jnp.where(count_half_ge(cand) >= rank, cand, t)
        return lax.fori_loop(0, 16, step, jnp.full((1, QB), I16_MIN, I32))

    t_hi = kth_largest_half(kk)
    above = jnp.where(t_hi < I16_MAX, count_half_ge(jnp.minimum(t_hi + 1, I16_MAX)), 0)

    def low_halves(c, carry):
        key = key_ref[c]
        low = (key & 0xFFFF) + I16_MIN
        half_ref[c] = jnp.where((key >> 16) == t_hi, low, I16_MIN).astype(I16)
        return carry

    lax.fori_loop(0, nck, low_halves, 0)
    t_lo = kth_largest_half(kk - above)
    thr = t_hi * 65536 + (t_lo - I16_MIN)
    n_ge = count(lambda c, k: jnp.where(k >= thr, 1.0, 0.0))
    surplus = jnp.max(jnp.where(thr > KEY_NEG_INF, n_ge - kk, 0.0))

    def break_ties():
        need = kk - count(lambda c, k: jnp.where(k > thr, 1.0, 0.0))

        def index_step(it, j):
            cand = j + lax.shift_left(jnp.int32(1), idx_bits - 1 - it)
            cnt = count(lambda c, k: jnp.where(k == thr, jnp.where(c * KC + row < cand, 1.0, 0.0), 0.0))
            return jnp.where(cnt < need, cand, j)
        return lax.fori_loop(0, idx_bits, index_step, jnp.zeros((1, QB), I32))

    jlim = lax.cond(surplus > 0.0, break_ties, lambda: jnp.full((1, QB), NO_TIE_LIMIT, I32))

    def write_mask(c, carry):
        k = key_ref[c]
        tie = jnp.where(c * KC + row <= jlim, 0.0, MASK_NEG)
        sel = jnp.where(k > thr, 0.0, jnp.where(k == thr, tie, MASK_NEG))
        mask = jnp.where(k > KEY_NEG_INF, sel, MASK_NEG)
        mb_ref[pl.ds(c * (KC // KS), KC // KS)] = mask.reshape(KC // KS, KS, QB)
        return carry

    lax.fori_loop(0, nck, write_mask, 0)

    qt = qt_ref[0] * ((HEAD_DIM ** -0.5) * LOG2E)
    zero_blk = jnp.zeros((HEAD_DIM, QB), BF16)
    for g in range(N_HEADS // HEAD_GROUP):
        for j in range(HEAD_GROUP):
            h = g * HEAD_GROUP + j
            qh = qt[h * HEAD_DIM:(h + 1) * HEAD_DIM, :].astype(BF16)
            qbd_ref[g, :, j * QB:(j + 1) * QB] = jnp.concatenate(
                [qh if r == j else zero_blk for r in range(HEAD_GROUP)], axis=0)
    m_ref[...] = jnp.full(m_ref.shape, MASK_NEG, F32)
    l_ref[...] = jnp.zeros(l_ref.shape, F32)
    acc_ref[...] = jnp.zeros((D_ATT, QB), F32)
    gl = HEAD_GROUP * QB
    tiles = lambda x: x.reshape(x.shape[0] // SUBLANES, SUBLANES, x.shape[1])
    gw = HEAD_GROUP * HEAD_DIM

    n_steps = (QB // KS) * nck

    def attend(c, band_sel):
        mbc = mb_ref[c]
        mask = jnp.concatenate([mbc] * HEAD_GROUP, axis=1)
        for g in range(N_HEADS // HEAD_GROUP):
            s = jnp.dot(k_ref[0, c, :, g * gw:(g + 1) * gw], qbd_ref[g], preferred_element_type=F32) + mask
            if band_sel is not None:
                s = s + band_ref[band_sel, g]
            s = tiles(s)
            m_old = m_ref[g]
            m_new = jnp.maximum(m_old, jnp.max(jnp.max(s, axis=0), axis=0, keepdims=True))
            alpha = jnp.exp2(m_old - m_new)
            p = jnp.exp2(s - m_new[None])
            l_ref[g] = alpha * l_ref[g] + jnp.sum(jnp.sum(p, axis=0), axis=0, keepdims=True)
            m_ref[g] = m_new
            p = p.reshape(KS, gl).astype(BF16)
            for j in range(HEAD_GROUP):
                h = g * HEAD_GROUP + j
                hs = slice(h * HEAD_DIM, (h + 1) * HEAD_DIM)
                qs = slice(j * QB, (j + 1) * QB)
                pv = jnp.dot(vt_ref[0, c, hs, :], p[:, qs], preferred_element_type=F32)
                acc_ref[hs, :] = (alpha[:, qs][None] * tiles(acc_ref[hs, :]) + tiles(pv)).reshape(HEAD_DIM, QB)

    def far_step(c, carry):
        attend(c, None)
        return carry

    n_far = jnp.maximum((QB // KS) * i - 1, 0)

    def near_step(c, carry):
        attend(c, c - (QB // KS) * i + 1)
        return carry

    lax.fori_loop(0, n_far, far_step, 0)
    lax.fori_loop(n_far, n_steps, near_step, 0)
    def normalised(h):
        l = l_ref[h // HEAD_GROUP][:, (h % HEAD_GROUP) * QB:(h % HEAD_GROUP + 1) * QB]
        return (tiles(acc_ref[h * HEAD_DIM:(h + 1) * HEAD_DIM, :]) / l[None]).reshape(HEAD_DIM, QB)

    for hp in range(N_HEADS // 2):
        pair = jnp.concatenate([normalised(2 * hp), normalised(2 * hp + 1)], axis=0)
        o_ref[0, :, hp * LANES:(hp + 1) * LANES] = pair.T


def _shifted_bias(rel_bias, buckets):
    far = rel_bias[N_BUCKETS - 1]
    lead = (N_HEADS,) + (1,) * buckets.ndim
    idx = jnp.asarray(buckets)[None]
    out = jnp.zeros((N_HEADS,) + buckets.shape, F32)
    for b in np.unique(buckets):
        if b != N_BUCKETS - 1:
            out = jnp.where(idx == b, (rel_bias[b] - far).reshape(lead), out)
    return out


def _prompt_bands(rel_bias):
    tab = _rel_bucket_table(QB + KS)
    sj = np.arange(KS)[:, None]
    qi = np.arange(QB)[None, :]
    buckets = np.stack([tab[np.maximum(qi - sj - (n - 1) * KS, 0)] for n in range(N_NEAR)])
    bias = _shifted_bias(rel_bias, buckets) * LOG2E
    bias = bias.reshape(N_HEADS // HEAD_GROUP, HEAD_GROUP, N_NEAR, KS, QB)
    return jnp.transpose(bias, (2, 0, 3, 1, 4)).reshape(N_NEAR, N_HEADS // HEAD_GROUP, KS, HEAD_GROUP * QB)


def _prompt_attention(qt, k, vtc, qit, kw, kwt, bands):
    b, t = qt.shape[0], qt.shape[2]
    nc = t // KC
    ns = t // KS
    kk = min(TOPK_MAX, t // 4)
    kc = k.reshape(b, ns, KS, D_ATT)
    ki = kw.reshape(b, nc, KC, LANES)
    col = lambda n: pl.BlockSpec((1, n, QB), lambda bi, i: (bi, 0, i))
    res = lambda s: pl.BlockSpec((1,) + s, lambda bi, i: (bi, 0, 0, 0))
    w_rows = pl.BlockSpec((1, SUBLANES, QB), lambda bi, i: (bi, D_IDX // SUBLANES, i))
    kern = functools.partial(_prompt_attn_kernel, kk=kk, idx_bits=max(1, (t - 1).bit_length()))
    return pl.pallas_call(
        kern,
        grid=(b, t // QB),
        in_specs=[col(D_ATT), res((ns, KS, D_ATT)), res((ns, D_ATT, KS)),
                  col(N_IDX_HEADS * D_IDX), res((nc, KC, LANES)), w_rows,
                  pl.BlockSpec(bands.shape, lambda bi, i: (0, 0, 0, 0))],
        out_specs=pl.BlockSpec((1, QB, D_ATT), lambda bi, i: (bi, i, 0)),
        out_shape=jax.ShapeDtypeStruct((b, t, D_ATT), F32),
        scratch_shapes=[pltpu.VMEM((nc, KC, QB), I32), pltpu.VMEM((nc, KC, QB), I16),
                        pltpu.VMEM((ns, KS, QB), F32),
                        pltpu.VMEM((N_HEADS // HEAD_GROUP, SUBLANES, HEAD_GROUP * QB), F32),
                        pltpu.VMEM((N_HEADS // HEAD_GROUP, SUBLANES, HEAD_GROUP * QB), F32),
                        pltpu.VMEM((N_HEADS // HEAD_GROUP, HEAD_GROUP * HEAD_DIM, HEAD_GROUP * QB), BF16),
                        pltpu.VMEM((D_ATT, QB), F32),
                        pltpu.VMEM((LANES, N_IDX_HEADS * QB), BF16)],
        compiler_params=_cparams(2),
        name="prompt_attention",
    )(qt, kc, vtc, qit, ki, kwt, bands)


def _s5_coeffs(lam_re, lam_im, log_step):
    lr = jnp.minimum(lam_re, -1e-4)
    li = lam_im
    dt = jnp.exp(log_step)
    mag = jnp.exp(lr * dt)
    a_re = mag * jnp.cos(li * dt)
    a_im = mag * jnp.sin(li * dt)
    den = lr * lr + li * li
    f_re = ((a_re - 1.0) * lr + a_im * li) / den
    f_im = (a_im * lr - (a_re - 1.0) * li) / den
    return a_re, a_im, f_re, f_im


def _s5_input_drive(u, f_re, f_im, bre_ref, bim_ref):
    w_re = f_re * bre_ref[...] - f_im * bim_ref[...]
    w_im = f_re * bim_ref[...] + f_im * bre_ref[...]
    ub = u.astype(BF16)
    return _bdot(ub, w_re), _bdot(ub, w_im)


def _s5_readout(h_re, h_im, u, gate, cre_ref, cim_ref, d_ref, gw_ref, gb_ref):
    y = _bdot(h_re, cre_ref[...]) - _bdot(h_im, cim_ref[...])
    y = y + d_ref[...] * u
    z = _gelu_tanh(y)
    z = z * _sigmoid(_bdot(z, gw_ref[...]) + gb_ref[...])
    return z * _silu(gate)


def _cmul(ar, ai, br, bi):
    return ar * br - ai * bi, ar * bi + ai * br


def _s5_seq_kernel(x_ref, h0re_ref, h0im_ref, lre_ref, lim_ref, ls_ref, bre_ref, bim_ref,
                   cre_ref, cim_ref, d_ref, gw_ref, gb_ref,
                   o_ref, hre_out, him_out, hre_s, him_s, cre_s, cim_s):
    c = pl.program_id(1)
    tc = x_ref.shape[1]

    @pl.when(c == 0)
    def _():
        cre_s[...] = h0re_ref[0]
        cim_s[...] = h0im_ref[0]

    u = x_ref[0, :, 0:D_SSM]
    gate = x_ref[0, :, D_SSM:]
    a_re, a_im, f_re, f_im = _s5_coeffs(lre_ref[...], lim_ref[...], ls_ref[...])
    bu_re, bu_im = _s5_input_drive(u, f_re, f_im, bre_ref, bim_ref)
    hre_s[...] = bu_re
    him_s[...] = bu_im

    full = lambda x: jnp.broadcast_to(x, (SUBLANES, S5_P))
    pows = [(full(a_re), full(a_im))]
    for _ in range(SUBLANES - 1):
        pows.append(_cmul(pows[-1][0], pows[-1][1], pows[0][0], pows[0][1]))
    row = lax.broadcasted_iota(I32, (SUBLANES, S5_P), 0)
    pw_re, pw_im = pows[SUBLANES - 1]
    for r in range(SUBLANES - 2, -1, -1):
        pw_re = jnp.where(row == r, pows[r][0], pw_re)
        pw_im = jnp.where(row == r, pows[r][1], pw_im)

    def tile(j, carry):
        cr, ci = carry
        sl = pl.ds(pl.multiple_of(j * SUBLANES, SUBLANES), SUBLANES)
        xr = hre_s[sl, :]
        xi = him_s[sl, :]
        for s in (1, 2, 4):
            sr = jnp.where(row >= s, pltpu.roll(xr, s, 0), 0.0)
            si = jnp.where(row >= s, pltpu.roll(xi, s, 0), 0.0)
            pr, pi = _cmul(pows[s - 1][0], pows[s - 1][1], sr, si)
            xr = xr + pr
            xi = xi + pi
        pr, pi = _cmul(pw_re, pw_im, cr, ci)
        xr = xr + pr
        xi = xi + pi
        hre_s[sl, :] = xr
        him_s[sl, :] = xi
        return xr[SUBLANES - 1:, :], xi[SUBLANES - 1:, :]

    cr, ci = lax.fori_loop(0, tc // SUBLANES, tile, (cre_s[...], cim_s[...]))
    cre_s[...] = cr
    cim_s[...] = ci
    hre_out[0] = cr
    him_out[0] = ci
    o_ref[0] = _s5_readout(hre_s[...], him_s[...], u, gate, cre_ref, cim_ref, d_ref, gw_ref, gb_ref)


def _s5_step_kernel(x_ref, h0re_ref, h0im_ref, lre_ref, lim_ref, ls_ref, bre_ref, bim_ref,
                    cre_ref, cim_ref, d_ref, gw_ref, gb_ref, o_ref, hre_out, him_out):
    u = x_ref[:, 0:D_SSM]
    gate = x_ref[:, D_SSM:]
    a_re, a_im, f_re, f_im = _s5_coeffs(lre_ref[...], lim_ref[...], ls_ref[...])
    bu_re, bu_im = _s5_input_drive(u, f_re, f_im, bre_ref, bim_ref)
    pr, pi = _cmul(a_re, a_im, h0re_ref[...], h0im_ref[...])
    h_re = bu_re + pr
    h_im = bu_im + pi
    hre_out[...] = h_re
    him_out[...] = h_im
    o_ref[...] = _s5_readout(h_re, h_im, u, gate, cre_ref, cim_ref, d_ref, gw_ref, gb_ref)


def _s5_params(lp):
    eye = jnp.eye(N_SSM_GROUPS, dtype=F32)

    def in_bd(b):
        return (jnp.transpose(b, (0, 2, 1))[:, :, None, :] * eye[:, None, :, None]).reshape(D_SSM, S5_P)

    def out_bd(c):
        return (jnp.transpose(c, (0, 2, 1))[:, :, None, :] * eye[:, None, :, None]).reshape(S5_P, D_SSM)

    ls = jnp.broadcast_to(lp['log_step'][:, None], (N_SSM_GROUPS, SSM_STATE))
    return (lp['lam_re'].reshape(1, S5_P), lp['lam_im'].reshape(1, S5_P), ls.reshape(1, S5_P),
            in_bd(lp['b_re']), in_bd(lp['b_im']),
            out_bd(lp['c_re']).astype(BF16), out_bd(lp['c_im']).astype(BF16),
            lp['d_skip'].reshape(1, D_SSM), lp['glu_w'].astype(BF16), lp['glu_b'].reshape(1, D_SSM))


def _s5_seq(x, h0_re, h0_im, params):
    b, t = x.shape[0], x.shape[1]
    tc = min(SCAN_CHUNK, t)
    full = lambda a: pl.BlockSpec(a.shape, lambda bi, c: (0,) * a.ndim)
    st = pl.BlockSpec((1, 1, S5_P), lambda bi, c: (bi, 0, 0))
    return pl.pallas_call(
        _s5_seq_kernel,
        grid=(b, t // tc),
        in_specs=[pl.BlockSpec((1, tc, 2 * D_SSM), lambda bi, c: (bi, c, 0)), st, st]
                 + [full(a) for a in params],
        out_specs=[pl.BlockSpec((1, tc, D_SSM), lambda bi, c: (bi, c, 0)), st, st],
        out_shape=[jax.ShapeDtypeStruct((b, t, D_SSM), F32),
                   jax.ShapeDtypeStruct((b, 1, S5_P), F32),
                   jax.ShapeDtypeStruct((b, 1, S5_P), F32)],
        scratch_shapes=[pltpu.VMEM((tc, S5_P), F32), pltpu.VMEM((tc, S5_P), F32),
                        pltpu.VMEM((1, S5_P), F32), pltpu.VMEM((1, S5_P), F32)],
        compiler_params=_cparams(2),
        name="s5_sequence",
    )(x, h0_re, h0_im, *params)


def _s5_step(x, h0_re, h0_im, params):
    b = x.shape[0]
    return pl.pallas_call(
        _s5_step_kernel,
        out_shape=[jax.ShapeDtypeStruct((b, D_SSM), F32),
                   jax.ShapeDtypeStruct((b, S5_P), F32),
                   jax.ShapeDtypeStruct((b, S5_P), F32)],
        compiler_params=pltpu.CompilerParams(vmem_limit_bytes=VMEM_LIMIT),
        name="s5_step",
    )(x, h0_re, h0_im, *params)


def _lru_gates(xc, wa_ref, ba_ref, wx_ref, bx_ref, lam_ref):
    xb = xc.astype(BF16)
    r = _sigmoid(_bdot(xb, wa_ref[...]) + ba_ref[...])
    g = _sigmoid(_bdot(xb, wx_ref[...]) + bx_ref[...])
    log_a = -LRU_C * r * _softplus(-lam_ref[...])
    a = jnp.exp(log_a)
    b = jnp.sqrt(-jnp.tanh(log_a) * (a * a + 1.0)) * (g * xc)
    return a, b


def _lru_seq_kernel(x_ref, conv0_ref, h0_ref, cw_ref, cb_ref, wa_ref, ba_ref, wx_ref, bx_ref, lam_ref,
                    o_ref, h_out, conv_out, ubuf, a_s, b_s, c_s):
    c = pl.program_id(1)
    tc = x_ref.shape[1]
    hist = CONV_W - 1

    @pl.when(c == 0)
    def _():
        ubuf[0:SUBLANES, :] = conv0_ref[0]
        c_s[...] = h0_ref[0]

    u = x_ref[0, :, 0:D_LRU]
    gate = x_ref[0, :, D_LRU:]
    ubuf[SUBLANES:, :] = u
    xc = cb_ref[...] + u * cw_ref[hist:hist + 1, :]
    for j in range(hist):
        xc = xc + ubuf[SUBLANES - hist + j:SUBLANES - hist + j + tc, :] * cw_ref[j:j + 1, :]
    tail = ubuf[tc:tc + SUBLANES, :]
    ubuf[0:SUBLANES, :] = tail
    conv_out[0] = tail

    a, b = _lru_gates(xc, wa_ref, ba_ref, wx_ref, bx_ref, lam_ref)
    a_s[...] = a
    b_s[...] = b
    row = lax.broadcasted_iota(I32, (SUBLANES, D_LRU), 0)

    def tile(j, carry):
        sl = pl.ds(pl.multiple_of(j * SUBLANES, SUBLANES), SUBLANES)
        av = a_s[sl, :]
        bv = b_s[sl, :]
        for s in (1, 2, 4):
            a_sh = jnp.where(row >= s, pltpu.roll(av, s, 0), 1.0)
            b_sh = jnp.where(row >= s, pltpu.roll(bv, s, 0), 0.0)
            bv = av * b_sh + bv
            av = av * a_sh
        h = bv + av * carry
        b_s[sl, :] = h
        return h[SUBLANES - 1:, :]

    carry = lax.fori_loop(0, tc // SUBLANES, tile, c_s[...])
    c_s[...] = carry
    h_out[0] = carry
    o_ref[0] = b_s[...] * _silu(gate)


def _lru_step_kernel(x_ref, c0_ref, c1_ref, c2_ref, h0_ref, cw_ref, cb_ref, wa_ref, ba_ref, wx_ref,
                     bx_ref, lam_ref, o_ref, h_out):
    u = x_ref[:, 0:D_LRU]
    gate = x_ref[:, D_LRU:]
    xc = (cb_ref[...] + c0_ref[...] * cw_ref[0:1, :] + c1_ref[...] * cw_ref[1:2, :]
          + c2_ref[...] * cw_ref[2:3, :] + u * cw_ref[3:4, :])
    a, b = _lru_gates(xc, wa_ref, ba_ref, wx_ref, bx_ref, lam_ref)
    h = a * h0_ref[...] + b
    h_out[...] = h
    o_ref[...] = h * _silu(gate)


def _lru_params(lp):
    eye = jnp.eye(N_LRU_BLOCKS, dtype=F32)
    bd = lambda w: (w[:, :, None, :] * eye[:, None, :, None]).reshape(D_LRU, D_LRU).astype(BF16)
    cw = jnp.concatenate([lp['conv_w'], jnp.zeros((SUBLANES - CONV_W, D_LRU), F32)], axis=0)
    return (cw, lp['conv_b'].reshape(1, D_LRU), bd(lp['wa']), lp['ba'].reshape(1, D_LRU),
            bd(lp['wx']), lp['bx'].reshape(1, D_LRU), lp['lam'].reshape(1, D_LRU))


def _lru_seq(x, conv0, h0, params):
    b, t = x.shape[0], x.shape[1]
    tc = min(SCAN_CHUNK, t)
    full = lambda a: pl.BlockSpec(a.shape, lambda bi, c: (0,) * a.ndim)
    return pl.pallas_call(
        _lru_seq_kernel,
        grid=(b, t // tc),
        in_specs=[pl.BlockSpec((1, tc, 2 * D_LRU), lambda bi, c: (bi, c, 0)),
                  pl.BlockSpec((1, SUBLANES, D_LRU), lambda bi, c: (bi, 0, 0)),
                  pl.BlockSpec((1, 1, D_LRU), lambda bi, c: (bi, 0, 0))]
                 + [full(a) for a in params],
        out_specs=[pl.BlockSpec((1, tc, D_LRU), lambda bi, c: (bi, c, 0)),
                   pl.BlockSpec((1, 1, D_LRU), lambda bi, c: (bi, 0, 0)),
                   pl.BlockSpec((1, SUBLANES, D_LRU), lambda bi, c: (bi, 0, 0))],
        out_shape=[jax.ShapeDtypeStruct((b, t, D_LRU), F32),
                   jax.ShapeDtypeStruct((b, 1, D_LRU), F32),
                   jax.ShapeDtypeStruct((b, SUBLANES, D_LRU), F32)],
        scratch_shapes=[pltpu.VMEM((tc + SUBLANES, D_LRU), F32), pltpu.VMEM((tc, D_LRU), F32),
                        pltpu.VMEM((tc, D_LRU), F32), pltpu.VMEM((1, D_LRU), F32)],
        compiler_params=_cparams(2),
        name="lru_sequence",
    )(x, conv0, h0, *params)


def _lru_step(x, conv0, h0, params):
    b = x.shape[0]
    return pl.pallas_call(
        _lru_step_kernel,
        out_shape=[jax.ShapeDtypeStruct((b, D_LRU), F32), jax.ShapeDtypeStruct((b, D_LRU), F32)],
        compiler_params=pltpu.CompilerParams(vmem_limit_bytes=VMEM_LIMIT),
        name="lru_step",
    )(x, conv0[:, 0], conv0[:, 1], conv0[:, 2], h0, *params)


def _dec_score_kernel(pt_ref, qi_ref, w_ref, *refs):
    pages, o_ref = refs[:-1], refs[-1]
    qi = qi_ref[0].astype(BF16)
    w = w_ref[0] * (N_IDX_HEADS ** -0.5)
    for g, page in enumerate(pages):
        d = jnp.dot(qi, page[0, 0].astype(BF16), preferred_element_type=F32) * (D_IDX ** -0.5)
        o_ref[0, 0, g:g + 1, :] = jnp.sum(w * jnp.maximum(d, 0.0), axis=0, keepdims=True)


def _dec_scores(q_idx, w_idx, kidx_t, page_table, layer):
    b = q_idx.shape[0]
    n_pages = page_table.shape[1]
    page = kidx_t.shape[3]
    g = math.gcd(SCORE_PAGES_PER_STEP, n_pages)
    page_spec = lambda j: pl.BlockSpec(
        (1, 1, D_IDX, page), lambda bi, p, pt: (layer, pt[bi * n_pages + p * g + j], 0, 0))
    out = pl.pallas_call(
        _dec_score_kernel,
        grid_spec=pltpu.PrefetchScalarGridSpec(
            num_scalar_prefetch=1,
            grid=(b, n_pages // g),
            in_specs=[pl.BlockSpec((1, N_IDX_HEADS, D_IDX), lambda bi, p, pt: (bi, 0, 0)),
                      pl.BlockSpec((1, N_IDX_HEADS, 1), lambda bi, p, pt: (bi, 0, 0))]
                     + [page_spec(j) for j in range(g)],
            out_specs=pl.BlockSpec((1, 1, g, page), lambda bi, p, pt: (bi, p, 0, 0)),
        ),
        out_shape=jax.ShapeDtypeStruct((b, n_pages // g, g, page), F32),
        compiler_params=_cparams(2),
        name="decode_scores",
    )(page_table.reshape(-1), q_idx.reshape(b, N_IDX_HEADS, D_IDX),
      w_idx.reshape(b, N_IDX_HEADS, 1), *([kidx_t] * g))
    return out.reshape(b, n_pages * page)


def _dec_select_kernel(s_ref, qi_ref, kn_ref, w_ref, mb_ref, mbn_ref, *, kk, idx_bits):
    b, past = s_ref.shape
    n = past + LANES
    qi = qi_ref[...]
    kn = kn_ref[...]
    w = w_ref[...] * (N_IDX_HEADS ** -0.5)
    s_new = jnp.zeros((b, 1), F32)
    for h in range(N_IDX_HEADS):
        d = jnp.sum(qi[:, h * D_IDX:(h + 1) * D_IDX] * kn, axis=1, keepdims=True) * (D_IDX ** -0.5)
        s_new = s_new + w[:, h:h + 1] * jnp.maximum(d, 0.0)
    lane = lax.broadcasted_iota(I32, (1, LANES), 1)
    tail = jnp.where(lane == 0, _sortable(jnp.broadcast_to(s_new, (b, LANES))), KEY_NEG_INF)
    keys = jnp.concatenate([_sortable(s_ref[...]), tail], axis=1)
    pos = lax.broadcasted_iota(I32, (1, n), 1)

    def count(m):
        return jnp.sum(m, axis=1, keepdims=True)

    def value_step(it, t):
        cand = t + lax.shift_left(jnp.int32(1), 31 - it)
        return jnp.where(count(jnp.where(keys >= cand, 1.0, 0.0)) >= kk, cand, t)

    thr = lax.fori_loop(0, 32, value_step, jnp.full((b, 1), INT_MIN, I32))
    need = kk - count(jnp.where(keys > thr, 1.0, 0.0))

    def index_step(it, j):
        cand = j + lax.shift_left(jnp.int32(1), idx_bits - 1 - it)
        c = count(jnp.where(keys == thr, jnp.where(pos < cand, 1.0, 0.0), 0.0))
        return jnp.where(c < need, cand, j)

    jlim = lax.fori_loop(0, idx_bits, index_step, jnp.zeros((b, 1), I32))
    tie = jnp.where(pos <= jlim, 0.0, MASK_NEG)
    sel = jnp.where(keys > thr, 0.0, jnp.where(keys == thr, tie, MASK_NEG))
    sel = jnp.where(keys > KEY_NEG_INF, sel, MASK_NEG)
    mb_ref[...] = sel[:, :past]
    mbn_ref[...] = sel[:, past:]


def _dec_select(scores, q_idx, k_idx_new, w_idx, kk):
    b, past = scores.shape
    kern = functools.partial(_dec_select_kernel, kk=kk, idx_bits=max(1, (past + LANES - 1).bit_length()))
    return pl.pallas_call(
        kern,
        out_shape=[jax.ShapeDtypeStruct((b, past), F32), jax.ShapeDtypeStruct((b, LANES), F32)],
        compiler_params=pltpu.CompilerParams(vmem_limit_bytes=VMEM_LIMIT),
        name="decode_select",
    )(scores, q_idx, k_idx_new, w_idx)


def _dec_attn_kernel(pt_ref, qb_ref, q_ref, kn_ref, vn_ref, mb_ref, mbn_ref, bias_ref, biasn_ref, *refs):
    g = (len(refs) - 4) // 2
    k_refs, v_refs = refs[:g], refs[g:2 * g]
    o_ref, m_s, l_s, acc_s = refs[2 * g:]
    step = pl.program_id(1)
    scale = HEAD_DIM ** -0.5

    @pl.when(step == 0)
    def _():
        m_s[...] = jnp.full(m_s.shape, MASK_NEG, F32)
        l_s[...] = jnp.zeros(l_s.shape, F32)
        acc_s[...] = jnp.zeros(acc_s.shape, F32)

    qb = qb_ref[0]
    s = jnp.concatenate([jnp.sum(kr[0, 0] * qb, axis=1) for kr in k_refs], axis=1)
    s = s * scale + bias_ref[...] + mb_ref[0]
    m_old = m_s[...]
    m_new = jnp.maximum(m_old, jnp.max(s, axis=1, keepdims=True))
    alpha = jnp.exp(m_old - m_new)
    p = jnp.exp(s - m_new)
    l_s[...] = alpha * l_s[...] + jnp.sum(p, axis=1, keepdims=True)
    m_s[...] = m_new
    page = qb.shape[2]
    for h in range(N_HEADS):
        upd = acc_s[h] * alpha[h:h + 1, :]
        for j, vr in enumerate(v_refs):
            upd = upd + p[h:h + 1, j * page:(j + 1) * page] * vr[0, 0, h]
        acc_s[h] = upd

    @pl.when(step == pl.num_programs(1) - 1)
    def _():
        ones = jnp.ones((1, page), F32)
        ctx = jnp.concatenate(
            [lax.dot_general(ones, acc_s[h], (((1,), (1,)), ((), ())), precision=HI,
                             preferred_element_type=F32) for h in range(N_HEADS)], axis=0)
        s_new = (jnp.sum(q_ref[0] * kn_ref[0], axis=1, keepdims=True) * scale
                 + biasn_ref[...] + mbn_ref[0][:, 0:1])
        m_f = jnp.maximum(m_s[...], s_new)
        a = jnp.exp(m_s[...] - m_f)
        pn = jnp.exp(s_new - m_f)
        o_ref[0] = (a * ctx + pn * vn_ref[0]) / (a * l_s[...] + pn)


def _dec_attention(q, k_new, v_new, mb, mb_new, bias, bias_new, ck_t, cv_t, page_table, layer):
    b = q.shape[0]
    n_pages = page_table.shape[1]
    page = ck_t.shape[4]
    g = math.gcd(ATTN_PAGES_PER_STEP, n_pages)
    steps = n_pages // g
    q3 = q.reshape(b, N_HEADS, HEAD_DIM)
    qb = jnp.broadcast_to(q3[:, :, :, None], (b, N_HEADS, HEAD_DIM, page))
    row = lambda: pl.BlockSpec((1, N_HEADS, HEAD_DIM), lambda bi, p, pt: (bi, 0, 0))
    page_spec = lambda j: pl.BlockSpec(
        (1, 1, N_HEADS, HEAD_DIM, page), lambda bi, p, pt: (layer, pt[bi * n_pages + p * g + j], 0, 0, 0))
    return pl.pallas_call(
        _dec_attn_kernel,
        grid_spec=pltpu.PrefetchScalarGridSpec(
            num_scalar_prefetch=1,
            grid=(b, steps),
            in_specs=[pl.BlockSpec((1, N_HEADS, HEAD_DIM, page), lambda bi, p, pt: (bi, 0, 0, 0)),
                      row(), row(), row(),
                      pl.BlockSpec((1, 1, g * page), lambda bi, p, pt: (bi, 0, p)),
                      pl.BlockSpec((1, 1, LANES), lambda bi, p, pt: (bi, 0, 0)),
                      pl.BlockSpec((N_HEADS, g * page), lambda bi, p, pt: (0, p)),
                      pl.BlockSpec((N_HEADS, 1), lambda bi, p, pt: (0, 0))]
                     + [page_spec(j) for j in range(g)] * 2,
            out_specs=row(),
            scratch_shapes=[pltpu.VMEM((N_HEADS, 1), F32), pltpu.VMEM((N_HEADS, 1), F32),
                            pltpu.VMEM((N_HEADS, HEAD_DIM, page), F32)],
        ),
        out_shape=jax.ShapeDtypeStruct((b, N_HEADS, HEAD_DIM), F32),
        compiler_params=_cparams(2),
        name="decode_attention",
    )(page_table.reshape(-1), qb, q3, k_new.reshape(b, N_HEADS, HEAD_DIM),
      v_new.reshape(b, N_HEADS, HEAD_DIM), mb.reshape(b, 1, n_pages * page),
      mb_new.reshape(b, 1, LANES), bias, bias_new, *([ck_t] * g), *([cv_t] * g))


def _decode_bias(rel_bias, past):
    tab = _rel_bucket_table(past + 1)
    bias = _shifted_bias(rel_bias, tab[past - np.arange(past)])
    return bias, (rel_bias[0] - rel_bias[N_BUCKETS - 1]).reshape(N_HEADS, 1)


def kernel(x_prompt, x_sample, cache_k, cache_v, cache_kidx, state_s5_re, state_s5_im, state_lru_h,
           state_lru_conv, page_table, w_in, w_out, ln_g, ln_b, rel_bias, s5_lam_re, s5_lam_im,
           s5_log_step, s5_b_re, s5_b_im, s5_c_re, s5_c_im, s5_d, glu_w, glu_b, lru_conv_w, lru_conv_b,
           lru_wa, lru_ba, lru_wx, lru_bx, lru_lam):
    depth = w_in.shape[0]
    bp, t = x_prompt.shape[0], x_prompt.shape[1]
    bs = x_sample.shape[0]
    n_pages = page_table.shape[1]
    page = cache_k.shape[2]
    past = n_pages * page
    kk_s = min(TOPK_MAX, (past + 1) // 4)
    bands = _prompt_bands(rel_bias)
    dec_bias, dec_bias_new = _decode_bias(rel_bias, past)
    hist = CONV_W - 1
    ck_t = jnp.transpose(cache_k, (0, 1, 3, 4, 2))
    cv_t = jnp.transpose(cache_v, (0, 1, 3, 4, 2))
    kidx_t = jnp.transpose(cache_kidx, (0, 1, 3, 2))
    w_in_t = jnp.transpose(w_in, (2, 0, 1))

    xp = x_prompt.reshape(bp * t, D_MODEL)
    xs = x_sample.reshape(bs, D_MODEL)
    zero_state = jnp.zeros((bp, 1, S5_P), F32)
    zero_h = jnp.zeros((bp, 1, D_LRU), F32)
    zero_conv = jnp.zeros((bp, SUBLANES, D_LRU), F32)
    new_p = [[] for _ in range(7)]
    new_s = [[] for _ in range(7)]

    for l in range(depth):
        w_packed = _pack_w_in_t(w_in_t[:, l, :])
        w_out_l = w_out[l].astype(BF16)
        s5p = _s5_params({'lam_re': s5_lam_re[l], 'lam_im': s5_lam_im[l], 'log_step': s5_log_step[l],
                          'b_re': s5_b_re[l], 'b_im': s5_b_im[l], 'c_re': s5_c_re[l], 'c_im': s5_c_im[l],
                          'd_skip': s5_d[l], 'glu_w': glu_w[l], 'glu_b': glu_b[l]})
        lrup = _lru_params({'conv_w': lru_conv_w[l], 'conv_b': lru_conv_b[l], 'wa': lru_wa[l],
                            'ba': lru_ba[l], 'wx': lru_wx[l], 'bx': lru_bx[l], 'lam': lru_lam[l]})
        lng = ln_g[l].reshape(1, D_MODEL)
        lnb = ln_b[l].reshape(1, D_MODEL)

        hp = _in_proj_prompt(xp.reshape(bp, t, D_MODEL), w_packed)
        o_att = _prompt_attention(hp['qT'], hp['k'], hp['vtc'], hp['qiT'], hp['kw'], hp['kwT'], bands)
        o_att = o_att.reshape(bp * t, D_ATT)
        o_ssm, s5re, s5im = _s5_seq(hp['ssm'], zero_state, zero_state, s5p)
        o_lru, lruh, conv = _lru_seq(hp['lru'], zero_conv, zero_h, lrup)
        flat = lambda a: a.reshape(bp * t, a.shape[-1])
        xp_new = _out_ln(xp, o_att, flat(hp['g']), flat(o_ssm), flat(o_lru), w_out_l, lng, lnb,
                         "out_ln_prompt")
        for lst, a in zip(new_p, (hp['kT'].reshape(bp, N_HEADS, HEAD_DIM, t),
                                  hp['vT'].reshape(bp, N_HEADS, HEAD_DIM, t),
                                  hp['kwT'][:, :D_IDX, :],
                                  s5re.reshape(bp, N_SSM_GROUPS, SSM_STATE),
                                  s5im.reshape(bp, N_SSM_GROUPS, SSM_STATE),
                                  lruh.reshape(bp, D_LRU), conv[:, SUBLANES - hist:, :])):
            lst.append(a)
        xp = xp_new

        hs = _in_proj(xs, w_packed, "in_proj_sample")
        kidx_s = hs['kw'][:, :D_IDX]
        widx_s = hs['kw'][:, D_IDX:D_IDX + N_IDX_HEADS]
        scores = _dec_scores(hs['qi'], widx_s, kidx_t, page_table, l)
        mb, mb_new = _dec_select(scores, hs['qi'], kidx_s, widx_s, kk_s)
        o_att_s = _dec_attention(hs['q'], hs['k'], hs['v'], mb, mb_new, dec_bias, dec_bias_new,
                                 ck_t, cv_t, page_table, l)
        o_ssm_s, s5re_s, s5im_s = _s5_step(hs['ssm'], state_s5_re[l].reshape(bs, S5_P),
                                           state_s5_im[l].reshape(bs, S5_P), s5p)
        conv0 = state_lru_conv[l]
        o_lru_s, lruh_s = _lru_step(hs['lru'], conv0, state_lru_h[l], lrup)
        conv_s = jnp.concatenate([conv0[:, 1:], hs['lru'][:, None, :D_LRU]], axis=1)
        xs_new = _out_ln(xs, o_att_s.reshape(bs, D_ATT), hs['g'], o_ssm_s, o_lru_s, w_out_l, lng, lnb,
                         "out_ln_sample")
        for lst, a in zip(new_s, (hs['k'].reshape(bs, 1, N_HEADS, HEAD_DIM),
                                  hs['v'].reshape(bs, 1, N_HEADS, HEAD_DIM),
                                  kidx_s.reshape(bs, 1, D_IDX),
                                  s5re_s.reshape(bs, N_SSM_GROUPS, SSM_STATE),
                                  s5im_s.reshape(bs, N_SSM_GROUPS, SSM_STATE),
                                  lruh_s, conv_s)):
            lst.append(a)
        xs = xs_new

    outs_p = [jnp.stack(a) for a in new_p]
    outs_p[0] = jnp.transpose(outs_p[0], (0, 1, 4, 2, 3))
    outs_p[1] = jnp.transpose(outs_p[1], (0, 1, 4, 2, 3))
    outs_p[2] = jnp.transpose(outs_p[2], (0, 1, 3, 2))
    outs_s = [jnp.stack(a) for a in new_s]
    return (xp.reshape(bp, t, D_MODEL), xs.reshape(bs, 1, D_MODEL), *outs_p, *outs_s)
```

```python
import functools
import math

import numpy as np
import jax
import jax.numpy as jnp
from jax import lax
from jax.experimental import pallas as pl
from jax.experimental.pallas import tpu as pltpu

F32 = jnp.float32
BF16 = jnp.bfloat16
I32 = jnp.int32
I16 = jnp.int16
I16_MIN = -2 ** 15
I16_MAX = 2 ** 15 - 1
HI = lax.Precision.HIGHEST

D_MODEL = 1024
D_ATT = 512
HEAD_DIM = 64
N_HEADS = 8
N_IDX_HEADS = 4
D_IDX = 64
TOPK_MAX = 256
N_BUCKETS = 32
MAX_DISTANCE = 128
D_SSM = 256
SSM_CH = 16
N_SSM_GROUPS = 16
SSM_STATE = 64
S5_P = N_SSM_GROUPS * SSM_STATE
D_LRU = 256
N_LRU_BLOCKS = 4
CONV_W = 4
LRU_C = 8.0
LN_EPS = 1e-5
DEPTH = 4
DEEPNORM_ALPHA = (2.0 * DEPTH) ** 0.25
LOG2E = math.log2(math.e)

SUBLANES = 8
LANES = 128
VMEM_LIMIT = 56 * 1024 * 1024

QB = 256
KC = 256
KS = 256
N_NEAR = QB // KS + 1
HEAD_GROUP = 4
SCAN_CHUNK = 512
SCORE_PAGES_PER_STEP = 64
ATTN_PAGES_PER_STEP = 16
MASK_NEG = -1e30
INT_MIN = -2 ** 31
KEY_NEG_INF = -2139095041
NO_TIE_LIMIT = 2 ** 30


def _cparams(n_axes, flags=None):
    return pltpu.CompilerParams(dimension_semantics=("arbitrary",) * n_axes,
                                vmem_limit_bytes=VMEM_LIMIT, flags=flags)


def _bdot(a, b):
    return jnp.dot(a.astype(BF16), b.astype(BF16), preferred_element_type=F32)


def _sigmoid(x):
    return 1.0 / (1.0 + jnp.exp(-x))


def _silu(x):
    return x * _sigmoid(x)


def _gelu_tanh(x):
    return 0.5 * x * (1.0 + jnp.tanh(math.sqrt(2.0 / math.pi) * (x + 0.044715 * (x * x * x))))


def _softplus(x):
    return jnp.maximum(x, 0.0) + jnp.log1p(jnp.exp(-jnp.abs(x)))


def _sortable(x):
    bits = pltpu.bitcast(x, I32)
    return bits ^ ((bits >> 31) & 0x7FFFFFFF)


def _rel_bucket_table(n):
    d = np.arange(n)
    exact = N_BUCKETS // 2
    far = exact + (np.log(np.maximum(d, exact).astype(np.float32) / exact)
                   / math.log(MAX_DISTANCE / exact) * (N_BUCKETS - exact)).astype(np.int32)
    return np.where(d < exact, d, np.minimum(far, N_BUCKETS - 1)).astype(np.int32)


IN_WIDTHS = dict(q=D_ATT, k=D_ATT, v=D_ATT, g=D_ATT, qi=N_IDX_HEADS * D_IDX, kw=LANES,
                 ssm=2 * D_SSM, lru=2 * D_LRU)


_offsets = np.concatenate([[0], np.cumsum(list(IN_WIDTHS.values()))]).tolist()
IN_ROWS = {name: (_offsets[i], _offsets[i + 1]) for i, name in enumerate(IN_WIDTHS)}
CONTRACT_LAST = (((1,), (1,)), ((), ()))


def _rows_major(x, wt_ref, name):
    lo, hi = IN_ROWS[name]
    return lax.dot_general(x, wt_ref[lo:hi, :], CONTRACT_LAST, preferred_element_type=F32)


def _feature_major(x, wt_ref, name):
    lo, hi = IN_ROWS[name]
    return lax.dot_general(wt_ref[lo:hi, :], x, CONTRACT_LAST, preferred_element_type=F32)


def _in_proj_kernel(x_ref, wt_ref, *o_refs):
    x = x_ref[...].astype(BF16)
    for name, o_ref in zip(IN_WIDTHS, o_refs):
        o_ref[...] = _rows_major(x, wt_ref, name)


def _pack_w_in_t(w_in_t):
    widths = (D_ATT, D_ATT, D_ATT, D_ATT, N_IDX_HEADS * D_IDX, D_IDX, N_IDX_HEADS,
              D_SSM, D_SSM, D_LRU, D_LRU)
    cut = int(np.sum(widths[:7]))
    pad = jnp.zeros((LANES - D_IDX - N_IDX_HEADS, D_MODEL), w_in_t.dtype)
    return jnp.concatenate([w_in_t[:cut], pad, w_in_t[cut:]], axis=0).astype(BF16)


def _in_proj(x2d, wt_packed, name):
    m = x2d.shape[0]
    tm = min(512, m)
    outs = pl.pallas_call(
        _in_proj_kernel,
        grid=(m // tm,),
        in_specs=[pl.BlockSpec((tm, D_MODEL), lambda i: (i, 0)),
                  pl.BlockSpec(wt_packed.shape, lambda i: (0, 0))],
        out_specs=[pl.BlockSpec((tm, n), lambda i: (i, 0)) for n in IN_WIDTHS.values()],
        out_shape=[jax.ShapeDtypeStruct((m, n), F32) for n in IN_WIDTHS.values()],
        compiler_params=_cparams(1),
        name=name,
    )(x2d, wt_packed)
    return dict(zip(IN_WIDTHS.keys(), outs))


IN_T_GROUPS = ('q', 'k', 'v', 'qi', 'kw')
IN_N_GROUPS = ('k', 'g', 'kw', 'ssm', 'lru')
IN_N_DTYPES = dict(k=BF16, g=F32, kw=BF16, ssm=F32, lru=F32)


def _in_proj_prompt_kernel(x_ref, wt_ref, *o_refs):
    nt_refs = o_refs[:len(IN_T_GROUPS)]
    vtc_ref = o_refs[len(IN_T_GROUPS)]
    nn_refs = o_refs[len(IN_T_GROUPS) + 1:]
    x = x_ref[0].astype(BF16)
    for name, o_ref in zip(IN_T_GROUPS, nt_refs):
        res = _feature_major(x, wt_ref, name)
        o_ref[0] = res
        if name == 'v':
            for j in range(vtc_ref.shape[1]):
                vtc_ref[0, j] = res[:, j * KS:(j + 1) * KS].astype(BF16)
    for name, o_ref in zip(IN_N_GROUPS, nn_refs):
        o_ref[0] = _rows_major(x, wt_ref, name).astype(o_ref.dtype)


def _in_proj_prompt(x, wt_packed):
    b, t = x.shape[0], x.shape[1]
    tm = min(512, t)
    t_spec = lambda n: pl.BlockSpec((1, n, tm), lambda bi, j: (bi, 0, j))
    n_spec = lambda n: pl.BlockSpec((1, tm, n), lambda bi, j: (bi, j, 0))
    outs = pl.pallas_call(
        _in_proj_prompt_kernel,
        grid=(b, t // tm),
        in_specs=[n_spec(D_MODEL),
                  pl.BlockSpec(wt_packed.shape, lambda bi, j: (0, 0))],
        out_specs=[t_spec(IN_WIDTHS[n]) for n in IN_T_GROUPS]
                  + [pl.BlockSpec((1, tm // KS, D_ATT, KS), lambda bi, j: (bi, j, 0, 0))]
                  + [n_spec(IN_WIDTHS[n]) for n in IN_N_GROUPS],
        out_shape=[jax.ShapeDtypeStruct((b, IN_WIDTHS[n], t), F32) for n in IN_T_GROUPS]
                  + [jax.ShapeDtypeStruct((b, t // KS, D_ATT, KS), BF16)]
                  + [jax.ShapeDtypeStruct((b, t, IN_WIDTHS[n]), IN_N_DTYPES[n]) for n in IN_N_GROUPS],
        compiler_params=_cparams(2),
        name="in_proj_prompt",
    )(x, wt_packed)
    res = {n + 'T': o for n, o in zip(IN_T_GROUPS, outs)}
    res['vtc'] = outs[len(IN_T_GROUPS)]
    res.update(zip(IN_N_GROUPS, outs[len(IN_T_GROUPS) + 1:]))
    return res


def _out_ln_kernel(x_ref, att_ref, gatt_ref, ssm_ref, lru_ref, w_ref, g_ref, b_ref, o_ref):
    att = att_ref[...] * _silu(gatt_ref[...])
    out = _bdot(att, w_ref[0:D_ATT, :])
    out += _bdot(ssm_ref[...], w_ref[D_ATT:D_ATT + D_SSM, :])
    out += _bdot(lru_ref[...], w_ref[D_ATT + D_SSM:, :])
    y = DEEPNORM_ALPHA * x_ref[...] + out
    mu = jnp.mean(y, axis=-1, keepdims=True)
    yc = y - mu
    var = jnp.mean(yc * yc, axis=-1, keepdims=True)
    o_ref[...] = yc * lax.rsqrt(var + LN_EPS) * g_ref[...] + b_ref[...]


def _out_ln(x, att, gatt, ssm, lru, w_out, ln_g, ln_b, name):
    m = x.shape[0]
    tm = min(512, m)
    row = lambda n: pl.BlockSpec((tm, n), lambda i: (i, 0))
    full = lambda a: pl.BlockSpec(a.shape, lambda i: (0,) * a.ndim)
    return pl.pallas_call(
        _out_ln_kernel,
        grid=(m // tm,),
        in_specs=[row(D_MODEL), row(D_ATT), row(D_ATT), row(D_SSM), row(D_LRU),
                  full(w_out), full(ln_g), full(ln_b)],
        out_specs=row(D_MODEL),
        out_shape=jax.ShapeDtypeStruct((m, D_MODEL), F32),
        compiler_params=_cparams(1),
        name=name,
    )(x, att, gatt, ssm, lru, w_out, ln_g, ln_b)


def _prompt_attn_kernel(qt_ref, k_ref, vt_ref, qit_ref, ki_ref, wt_ref, band_ref, o_ref,
                        key_ref, half_ref, mb_ref, m_ref, l_ref, qbd_ref, acc_ref, qcat_ref, *, kk, idx_bits):
    i = pl.program_id(1)
    nck = i + 1
    qit = qit_ref[0].astype(BF16)
    zero_rows = jnp.zeros((LANES - D_IDX, QB), BF16)
    for h in range(N_IDX_HEADS):
        qcat_ref[:, h * QB:(h + 1) * QB] = jnp.concatenate(
            [qit[h * D_IDX:(h + 1) * D_IDX, :], zero_rows], axis=0)
    w = wt_ref[0] * (N_IDX_HEADS ** -0.5 * D_IDX ** -0.5)
    qpos = i * QB + lax.broadcasted_iota(I32, (1, QB), 1)
    row = lax.broadcasted_iota(I32, (KC, 1), 0)

    def score_chunk(c, carry):
        d = jnp.dot(ki_ref[0, c], qcat_ref[...], preferred_element_type=F32)
        s = jnp.zeros((KC, QB), F32)
        for h in range(N_IDX_HEADS):
            s = s + w[h:h + 1, :] * jnp.maximum(d[:, h * QB:(h + 1) * QB], 0.0)
        key = jnp.where(c * KC + row <= qpos, _sortable(s), KEY_NEG_INF)
        key_ref[c] = key
        half_ref[c] = (key >> 16).astype(I16)
        return carry

    lax.fori_loop(0, nck, score_chunk, 0)

    def count(pred):
        def body(c, acc):
            hit = pred(c, key_ref[c])
            return acc + jnp.sum(hit.reshape(KC // SUBLANES, SUBLANES, QB), axis=0)
        acc = lax.fori_loop(0, nck, body, jnp.zeros((SUBLANES, QB), F32))
        return jnp.sum(acc, axis=0, keepdims=True)

    half_tile = 2 * SUBLANES

    def count_half_ge(cand):
        cand16 = cand.astype(I16)

        def body(c, accs):
            hit = jnp.where(half_ref[c] >= cand16, jnp.int16(1), jnp.int16(0))
            hit = hit.reshape(KC // half_tile, half_tile, QB)
            accs = list(accs)
            for r in range(KC // half_tile):
                accs[r % len(accs)] = accs[r % len(accs)] + hit[r]
            return tuple(accs)

        zero = jnp.zeros((half_tile, QB), I16)
        accs = lax.fori_loop(0, nck, body, (zero,) * 4)
        acc = (accs[0] + accs[1]) + (accs[2] + accs[3])
        return jnp.sum(acc.astype(I32), axis=0, keepdims=True)

    def kth_largest_half(rank):
        def step(it, t):
            cand = t + lax.shift_left(jnp.int32(1), 15 - it)
            return jnp.where(count_half_ge(cand) >= rank, cand, t)
        return lax.fori_loop(0, 16, step, jnp.full((1, QB), I16_MIN, I32))

    t_hi = kth_largest_half(kk)
    above = jnp.where(t_hi < I16_MAX, count_half_ge(jnp.minimum(t_hi + 1, I16_MAX)), 0)

    def low_halves(c, carry):
        key = key_ref[c]
        low = (key & 0xFFFF) + I16_MIN
        half_ref[c] = jnp.where((key >> 16) == t_hi, low, I16_MIN).astype(I16)
        return carry

    lax.fori_loop(0, nck, low_halves, 0)
    t_lo = kth_largest_half(kk - above)
    thr = t_hi * 65536 + (t_lo - I16_MIN)
    n_ge = count(lambda c, k: jnp.where(k >= thr, 1.0, 0.0))
    surplus = jnp.max(jnp.where(thr > KEY_NEG_INF, n_ge - kk, 0.0))

    def break_ties():
        need = kk - count(lambda c, k: jnp.where(k > thr, 1.0, 0.0))

        def index_step(it, j):
            cand = j + lax.shift_left(jnp.int32(1), idx_bits - 1 - it)
            cnt = count(lambda c, k: jnp.where(k == thr, jnp.where(c * KC + row < cand, 1.0, 0.0), 0.0))
            return jnp.where(cnt < need, cand, j)
        return lax.fori_loop(0, idx_bits, index_step, jnp.zeros((1, QB), I32))

    jlim = lax.cond(surplus > 0.0, break_ties, lambda: jnp.full((1, QB), NO_TIE_LIMIT, I32))

    def write_mask(c, carry):
        k = key_ref[c]
        tie = jnp.where(c * KC + row <= jlim, 0.0, MASK_NEG)
        sel = jnp.where(k > thr, 0.0, jnp.where(k == thr, tie, MASK_NEG))
        mask = jnp.where(k > KEY_NEG_INF, sel, MASK_NEG)
        mb_ref[pl.ds(c * (KC // KS), KC // KS)] = mask.reshape(KC // KS, KS, QB)
        return carry

    lax.fori_loop(0, nck, write_mask, 0)

    qt = qt_ref[0] * ((HEAD_DIM ** -0.5) * LOG2E)
    zero_blk = jnp.zeros((HEAD_DIM, QB), BF16)
    for g in range(N_HEADS // HEAD_GROUP):
        for j in range(HEAD_GROUP):
            h = g * HEAD_GROUP + j
            qh = qt[h * HEAD_DIM:(h + 1) * HEAD_DIM, :].astype(BF16)
            qbd_ref[g, :, j * QB:(j + 1) * QB] = jnp.concatenate(
                [qh if r == j else zero_blk for r in range(HEAD_GROUP)], axis=0)
    m_ref[...] = jnp.full(m_ref.shape, MASK_NEG, F32)
    l_ref[...] = jnp.zeros(l_ref.shape, F32)
    acc_ref[...] = jnp.zeros((D_ATT, QB), F32)
    gl = HEAD_GROUP * QB
    ones_rows = jnp.ones((2 * SUBLANES, KS), BF16)
    tiles = lambda x: x.reshape(x.shape[0] // SUBLANES, SUBLANES, x.shape[1])
    gw = HEAD_GROUP * HEAD_DIM

    n_steps = (QB // KS) * nck

    def attend(c, band_sel):
        mbc = mb_ref[c]
        mask = jnp.concatenate([mbc] * HEAD_GROUP, axis=1)
        for g in range(N_HEADS // HEAD_GROUP):
            s = jnp.dot(k_ref[0, c, :, g * gw:(g + 1) * gw], qbd_ref[g], preferred_element_type=F32) + mask
            if band_sel is not None:
                s = s + band_ref[band_sel, g]
            s = tiles(s)
            m_old = m_ref[g]
            m_new = jnp.maximum(m_old, jnp.max(jnp.max(s, axis=0), axis=0, keepdims=True))
            alpha = jnp.exp2(m_old - m_new)
            p = jnp.exp2((s - m_new[None]).reshape(KS, gl).astype(BF16))
            m_ref[g] = m_new
            sums = []
            for j in range(HEAD_GROUP):
                h = g * HEAD_GROUP + j
                hs = slice(h * HEAD_DIM, (h + 1) * HEAD_DIM)
                qs = slice(j * QB, (j + 1) * QB)
                pv = jnp.dot(jnp.concatenate([vt_ref[0, c, hs, :], ones_rows], axis=0), p[:, qs],
                             preferred_element_type=F32)
                acc_ref[hs, :] = (alpha[:, qs][None] * tiles(acc_ref[hs, :])
                                  + tiles(pv[:HEAD_DIM])).reshape(HEAD_DIM, QB)
                sums.append(pv[HEAD_DIM:HEAD_DIM + SUBLANES])
            l_ref[g] = alpha * l_ref[g] + jnp.concatenate(sums, axis=1)

    def far_step(c, carry):
        attend(c, None)
        return carry

    n_far = jnp.maximum((QB // KS) * i - 1, 0)

    def near_step(c, carry):
        attend(c, c - (QB // KS) * i + 1)
        return carry

    lax.fori_loop(0, n_far, far_step, 0)
    lax.fori_loop(n_far, n_steps, near_step, 0)
    def normalised(h):
        l = l_ref[h // HEAD_GROUP][:, (h % HEAD_GROUP) * QB:(h % HEAD_GROUP + 1) * QB]
        return (tiles(acc_ref[h * HEAD_DIM:(h + 1) * HEAD_DIM, :]) / l[None]).reshape(HEAD_DIM, QB)

    for hp in range(N_HEADS // 2):
        pair = jnp.concatenate([normalised(2 * hp), normalised(2 * hp + 1)], axis=0)
        o_ref[0, :, hp * LANES:(hp + 1) * LANES] = pair.T


def _shifted_bias(rel_bias, buckets):
    far = rel_bias[N_BUCKETS - 1]
    lead = (N_HEADS,) + (1,) * buckets.ndim
    idx = jnp.asarray(buckets)[None]
    out = jnp.zeros((N_HEADS,) + buckets.shape, F32)
    for b in np.unique(buckets):
        if b != N_BUCKETS - 1:
            out = jnp.where(idx == b, (rel_bias[b] - far).reshape(lead), out)
    return out


def _prompt_bands(rel_bias):
    tab = _rel_bucket_table(QB + KS)
    sj = np.arange(KS)[:, None]
    qi = np.arange(QB)[None, :]
    buckets = np.stack([tab[np.maximum(qi - sj - (n - 1) * KS, 0)] for n in range(N_NEAR)])
    bias = _shifted_bias(rel_bias, buckets) * LOG2E
    bias = bias.reshape(N_HEADS // HEAD_GROUP, HEAD_GROUP, N_NEAR, KS, QB)
    return jnp.transpose(bias, (2, 0, 3, 1, 4)).reshape(N_NEAR, N_HEADS // HEAD_GROUP, KS, HEAD_GROUP * QB)


def _prompt_attention(qt, k, vtc, qit, kw, kwt, bands):
    b, t = qt.shape[0], qt.shape[2]
    nc = t // KC
    ns = t // KS
    kk = min(TOPK_MAX, t // 4)
    kc = k.reshape(b, ns, KS, D_ATT)
    ki = kw.reshape(b, nc, KC, LANES)
    col = lambda n: pl.BlockSpec((1, n, QB), lambda bi, i: (bi, 0, i))
    res = lambda s: pl.BlockSpec((1,) + s, lambda bi, i: (bi, 0, 0, 0))
    w_rows = pl.BlockSpec((1, SUBLANES, QB), lambda bi, i: (bi, D_IDX // SUBLANES, i))
    kern = functools.partial(_prompt_attn_kernel, kk=kk, idx_bits=max(1, (t - 1).bit_length()))
    return pl.pallas_call(
        kern,
        grid=(b, t // QB),
        in_specs=[col(D_ATT), res((ns, KS, D_ATT)), res((ns, D_ATT, KS)),
                  col(N_IDX_HEADS * D_IDX), res((nc, KC, LANES)), w_rows,
                  pl.BlockSpec(bands.shape, lambda bi, i: (0, 0, 0, 0))],
        out_specs=pl.BlockSpec((1, QB, D_ATT), lambda bi, i: (bi, i, 0)),
        out_shape=jax.ShapeDtypeStruct((b, t, D_ATT), F32),
        scratch_shapes=[pltpu.VMEM((nc, KC, QB), I32), pltpu.VMEM((nc, KC, QB), I16),
                        pltpu.VMEM((ns, KS, QB), F32),
                        pltpu.VMEM((N_HEADS // HEAD_GROUP, SUBLANES, HEAD_GROUP * QB), F32),
                        pltpu.VMEM((N_HEADS // HEAD_GROUP, SUBLANES, HEAD_GROUP * QB), F32),
                        pltpu.VMEM((N_HEADS // HEAD_GROUP, HEAD_GROUP * HEAD_DIM, HEAD_GROUP * QB), BF16),
                        pltpu.VMEM((D_ATT, QB), F32),
                        pltpu.VMEM((LANES, N_IDX_HEADS * QB), BF16)],
        compiler_params=_cparams(2),
        name="prompt_attention",
    )(qt, kc, vtc, qit, ki, kwt, bands)


def _s5_coeffs(lam_re, lam_im, log_step):
    lr = jnp.minimum(lam_re, -1e-4)
    li = lam_im
    dt = jnp.exp(log_step)
    mag = jnp.exp(lr * dt)
    a_re = mag * jnp.cos(li * dt)
    a_im = mag * jnp.sin(li * dt)
    den = lr * lr + li * li
    f_re = ((a_re - 1.0) * lr + a_im * li) / den
    f_im = (a_im * lr - (a_re - 1.0) * li) / den
    return a_re, a_im, f_re, f_im


def _s5_input_drive(u, f_re, f_im, bre_ref, bim_ref):
    w_re = f_re * bre_ref[...] - f_im * bim_ref[...]
    w_im = f_re * bim_ref[...] + f_im * bre_ref[...]
    ub = u.astype(BF16)
    return _bdot(ub, w_re), _bdot(ub, w_im)


def _s5_readout(h_re, h_im, u, gate, cre_ref, cim_ref, d_ref, gw_ref, gb_ref):
    y = _bdot(h_re, cre_ref[...]) - _bdot(h_im, cim_ref[...])
    y = y + d_ref[...] * u
    z = _gelu_tanh(y)
    z = z * _sigmoid(_bdot(z, gw_ref[...]) + gb_ref[...])
    return z * _silu(gate)


def _cmul(ar, ai, br, bi):
    return ar * br - ai * bi, ar * bi + ai * br


def _s5_seq_kernel(x_ref, h0re_ref, h0im_ref, lre_ref, lim_ref, ls_ref, bre_ref, bim_ref,
                   cre_ref, cim_ref, d_ref, gw_ref, gb_ref,
                   o_ref, hre_out, him_out, hre_s, him_s, cre_s, cim_s):
    c = pl.program_id(1)
    tc = x_ref.shape[1]

    @pl.when(c == 0)
    def _():
        cre_s[...] = h0re_ref[0]
        cim_s[...] = h0im_ref[0]

    u = x_ref[0, :, 0:D_SSM]
    gate = x_ref[0, :, D_SSM:]
    a_re, a_im, f_re, f_im = _s5_coeffs(lre_ref[...], lim_ref[...], ls_ref[...])
    bu_re, bu_im = _s5_input_drive(u, f_re, f_im, bre_ref, bim_ref)
    hre_s[...] = bu_re
    him_s[...] = bu_im

    full = lambda x: jnp.broadcast_to(x, (SUBLANES, S5_P))
    pows = [(full(a_re), full(a_im))]
    for _ in range(SUBLANES - 1):
        pows.append(_cmul(pows[-1][0], pows[-1][1], pows[0][0], pows[0][1]))
    row = lax.broadcasted_iota(I32, (SUBLANES, S5_P), 0)
    pw_re, pw_im = pows[SUBLANES - 1]
    for r in range(SUBLANES - 2, -1, -1):
        pw_re = jnp.where(row == r, pows[r][0], pw_re)
        pw_im = jnp.where(row == r, pows[r][1], pw_im)

    shifted = {s: (jnp.where(row >= s, pows[s - 1][0], 0.0), jnp.where(row >= s, pows[s - 1][1], 0.0))
               for s in (1, 2, 4)}

    def tile(j, carry):
        cr, ci = carry
        sl = pl.ds(pl.multiple_of(j * SUBLANES, SUBLANES), SUBLANES)
        xr = hre_s[sl, :]
        xi = him_s[sl, :]
        for s in (1, 2, 4):
            pr, pi = _cmul(shifted[s][0], shifted[s][1], pltpu.roll(xr, s, 0), pltpu.roll(xi, s, 0))
            xr = xr + pr
            xi = xi + pi
        pr, pi = _cmul(pw_re, pw_im, cr, ci)
        xr = xr + pr
        xi = xi + pi
        hre_s[sl, :] = xr
        him_s[sl, :] = xi
        return xr[SUBLANES - 1:, :], xi[SUBLANES - 1:, :]

    cr, ci = lax.fori_loop(0, tc // SUBLANES, tile, (cre_s[...], cim_s[...]))
    cre_s[...] = cr
    cim_s[...] = ci
    hre_out[0] = cr
    him_out[0] = ci
    o_ref[0] = _s5_readout(hre_s[...], him_s[...], u, gate, cre_ref, cim_ref, d_ref, gw_ref, gb_ref)


def _s5_step_kernel(x_ref, h0re_ref, h0im_ref, lre_ref, lim_ref, ls_ref, bre_ref, bim_ref,
                    cre_ref, cim_ref, d_ref, gw_ref, gb_ref, o_ref, hre_out, him_out):
    u = x_ref[:, 0:D_SSM]
    gate = x_ref[:, D_SSM:]
    a_re, a_im, f_re, f_im = _s5_coeffs(lre_ref[...], lim_ref[...], ls_ref[...])
    bu_re, bu_im = _s5_input_drive(u, f_re, f_im, bre_ref, bim_ref)
    pr, pi = _cmul(a_re, a_im, h0re_ref[...], h0im_ref[...])
    h_re = bu_re + pr
    h_im = bu_im + pi
    hre_out[...] = h_re
    him_out[...] = h_im
    o_ref[...] = _s5_readout(h_re, h_im, u, gate, cre_ref, cim_ref, d_ref, gw_ref, gb_ref)


def _s5_params(lp):
    eye = jnp.eye(N_SSM_GROUPS, dtype=F32)

    def in_bd(b):
        return (jnp.transpose(b, (0, 2, 1))[:, :, None, :] * eye[:, None, :, None]).reshape(D_SSM, S5_P)

    def out_bd(c):
        return (jnp.transpose(c, (0, 2, 1))[:, :, None, :] * eye[:, None, :, None]).reshape(S5_P, D_SSM)

    ls = jnp.broadcast_to(lp['log_step'][:, None], (N_SSM_GROUPS, SSM_STATE))
    return (lp['lam_re'].reshape(1, S5_P), lp['lam_im'].reshape(1, S5_P), ls.reshape(1, S5_P),
            in_bd(lp['b_re']), in_bd(lp['b_im']),
            out_bd(lp['c_re']).astype(BF16), out_bd(lp['c_im']).astype(BF16),
            lp['d_skip'].reshape(1, D_SSM), lp['glu_w'].astype(BF16), lp['glu_b'].reshape(1, D_SSM))


def _s5_seq(x, h0_re, h0_im, params):
    b, t = x.shape[0], x.shape[1]
    tc = min(SCAN_CHUNK, t)
    full = lambda a: pl.BlockSpec(a.shape, lambda bi, c: (0,) * a.ndim)
    st = pl.BlockSpec((1, 1, S5_P), lambda bi, c: (bi, 0, 0))
    return pl.pallas_call(
        _s5_seq_kernel,
        grid=(b, t // tc),
        in_specs=[pl.BlockSpec((1, tc, 2 * D_SSM), lambda bi, c: (bi, c, 0)), st, st]
                 + [full(a) for a in params],
        out_specs=[pl.BlockSpec((1, tc, D_SSM), lambda bi, c: (bi, c, 0)), st, st],
        out_shape=[jax.ShapeDtypeStruct((b, t, D_SSM), F32),
                   jax.ShapeDtypeStruct((b, 1, S5_P), F32),
                   jax.ShapeDtypeStruct((b, 1, S5_P), F32)],
        scratch_shapes=[pltpu.VMEM((tc, S5_P), F32), pltpu.VMEM((tc, S5_P), F32),
                        pltpu.VMEM((1, S5_P), F32), pltpu.VMEM((1, S5_P), F32)],
        compiler_params=_cparams(2),
        name="s5_sequence",
    )(x, h0_re, h0_im, *params)


def _s5_step(x, h0_re, h0_im, params):
    b = x.shape[0]
    return pl.pallas_call(
        _s5_step_kernel,
        out_shape=[jax.ShapeDtypeStruct((b, D_SSM), F32),
                   jax.ShapeDtypeStruct((b, S5_P), F32),
                   jax.ShapeDtypeStruct((b, S5_P), F32)],
        compiler_params=pltpu.CompilerParams(vmem_limit_bytes=VMEM_LIMIT),
        name="s5_step",
    )(x, h0_re, h0_im, *params)


def _lru_gates(xc, wa_ref, ba_ref, wx_ref, bx_ref, lam_ref):
    xb = xc.astype(BF16)
    r = _sigmoid(_bdot(xb, wa_ref[...]) + ba_ref[...])
    g = _sigmoid(_bdot(xb, wx_ref[...]) + bx_ref[...])
    log_a = -LRU_C * r * _softplus(-lam_ref[...])
    a = jnp.exp(log_a)
    b = jnp.sqrt(-jnp.tanh(log_a) * (a * a + 1.0)) * (g * xc)
    return a, b


def _lru_seq_kernel(x_ref, conv0_ref, h0_ref, cw_ref, cb_ref, wa_ref, ba_ref, wx_ref, bx_ref, lam_ref,
                    o_ref, h_out, conv_out, ubuf, a_s, b_s, c_s):
    c = pl.program_id(1)
    tc = x_ref.shape[1]
    hist = CONV_W - 1

    @pl.when(c == 0)
    def _():
        ubuf[0:SUBLANES, :] = conv0_ref[0]
        c_s[...] = h0_ref[0]

    u = x_ref[0, :, 0:D_LRU]
    gate = x_ref[0, :, D_LRU:]
    ubuf[SUBLANES:, :] = u
    xc = cb_ref[...] + u * cw_ref[hist:hist + 1, :]
    for j in range(hist):
        xc = xc + ubuf[SUBLANES - hist + j:SUBLANES - hist + j + tc, :] * cw_ref[j:j + 1, :]
    tail = ubuf[tc:tc + SUBLANES, :]
    ubuf[0:SUBLANES, :] = tail
    conv_out[0] = tail

    a, b = _lru_gates(xc, wa_ref, ba_ref, wx_ref, bx_ref, lam_ref)
    a_s[...] = a
    b_s[...] = b
    row = lax.broadcasted_iota(I32, (SUBLANES, D_LRU), 0)

    def tile(j, carry):
        sl = pl.ds(pl.multiple_of(j * SUBLANES, SUBLANES), SUBLANES)
        av = a_s[sl, :]
        bv = b_s[sl, :]
        for s in (1, 2, 4):
            a_sh = jnp.where(row >= s, pltpu.roll(av, s, 0), 1.0)
            b_sh = jnp.where(row >= s, pltpu.roll(bv, s, 0), 0.0)
            bv = av * b_sh + bv
            av = av * a_sh
        h = bv + av * carry
        b_s[sl, :] = h
        return h[SUBLANES - 1:, :]

    carry = lax.fori_loop(0, tc // SUBLANES, tile, c_s[...])
    c_s[...] = carry
    h_out[0] = carry
    o_ref[0] = b_s[...] * _silu(gate)


def _lru_step_kernel(x_ref, c0_ref, c1_ref, c2_ref, h0_ref, cw_ref, cb_ref, wa_ref, ba_ref, wx_ref,
                     bx_ref, lam_ref, o_ref, h_out):
    u = x_ref[:, 0:D_LRU]
    gate = x_ref[:, D_LRU:]
    xc = (cb_ref[...] + c0_ref[...] * cw_ref[0:1, :] + c1_ref[...] * cw_ref[1:2, :]
          + c2_ref[...] * cw_ref[2:3, :] + u * cw_ref[3:4, :])
    a, b = _lru_gates(xc, wa_ref, ba_ref, wx_ref, bx_ref, lam_ref)
    h = a * h0_ref[...] + b
    h_out[...] = h
    o_ref[...] = h * _silu(gate)


def _lru_params(lp):
    eye = jnp.eye(N_LRU_BLOCKS, dtype=F32)
    bd = lambda w: (w[:, :, None, :] * eye[:, None, :, None]).reshape(D_LRU, D_LRU).astype(BF16)
    cw = jnp.concatenate([lp['conv_w'], jnp.zeros((SUBLANES - CONV_W, D_LRU), F32)], axis=0)
    return (cw, lp['conv_b'].reshape(1, D_LRU), bd(lp['wa']), lp['ba'].reshape(1, D_LRU),
            bd(lp['wx']), lp['bx'].reshape(1, D_LRU), lp['lam'].reshape(1, D_LRU))


def _lru_seq(x, conv0, h0, params):
    b, t = x.shape[0], x.shape[1]
    tc = min(SCAN_CHUNK, t)
    full = lambda a: pl.BlockSpec(a.shape, lambda bi, c: (0,) * a.ndim)
    return pl.pallas_call(
        _lru_seq_kernel,
        grid=(b, t // tc),
        in_specs=[pl.BlockSpec((1, tc, 2 * D_LRU), lambda bi, c: (bi, c, 0)),
                  pl.BlockSpec((1, SUBLANES, D_LRU), lambda bi, c: (bi, 0, 0)),
                  pl.BlockSpec((1, 1, D_LRU), lambda bi, c: (bi, 0, 0))]
                 + [full(a) for a in params],
        out_specs=[pl.BlockSpec((1, tc, D_LRU), lambda bi, c: (bi, c, 0)),
                   pl.BlockSpec((1, 1, D_LRU), lambda bi, c: (bi, 0, 0)),
                   pl.BlockSpec((1, SUBLANES, D_LRU), lambda bi, c: (bi, 0, 0))],
        out_shape=[jax.ShapeDtypeStruct((b, t, D_LRU), F32),
                   jax.ShapeDtypeStruct((b, 1, D_LRU), F32),
                   jax.ShapeDtypeStruct((b, SUBLANES, D_LRU), F32)],
        scratch_shapes=[pltpu.VMEM((tc + SUBLANES, D_LRU), F32), pltpu.VMEM((tc, D_LRU), F32),
                        pltpu.VMEM((tc, D_LRU), F32), pltpu.VMEM((1, D_LRU), F32)],
        compiler_params=_cparams(2),
        name="lru_sequence",
    )(x, conv0, h0, *params)


def _lru_step(x, conv0, h0, params):
    b = x.shape[0]
    return pl.pallas_call(
        _lru_step_kernel,
        out_shape=[jax.ShapeDtypeStruct((b, D_LRU), F32), jax.ShapeDtypeStruct((b, D_LRU), F32)],
        compiler_params=pltpu.CompilerParams(vmem_limit_bytes=VMEM_LIMIT),
        name="lru_step",
    )(x, conv0[:, 0], conv0[:, 1], conv0[:, 2], h0, *params)


def _dec_score_kernel(pt_ref, qi_ref, w_ref, *refs):
    pages, o_ref = refs[:-1], refs[-1]
    qi = qi_ref[0].astype(BF16)
    w = w_ref[0] * (N_IDX_HEADS ** -0.5)
    for g, page in enumerate(pages):
        d = jnp.dot(qi, page[0, 0].astype(BF16), preferred_element_type=F32) * (D_IDX ** -0.5)
        o_ref[0, 0, g:g + 1, :] = jnp.sum(w * jnp.maximum(d, 0.0), axis=0, keepdims=True)


def _dec_scores(q_idx, w_idx, kidx_t, page_table, layer):
    b = q_idx.shape[0]
    n_pages = page_table.shape[1]
    page = kidx_t.shape[3]
    g = math.gcd(SCORE_PAGES_PER_STEP, n_pages)
    page_spec = lambda j: pl.BlockSpec(
        (1, 1, D_IDX, page), lambda bi, p, pt: (layer, pt[bi * n_pages + p * g + j], 0, 0))
    out = pl.pallas_call(
        _dec_score_kernel,
        grid_spec=pltpu.PrefetchScalarGridSpec(
            num_scalar_prefetch=1,
            grid=(b, n_pages // g),
            in_specs=[pl.BlockSpec((1, N_IDX_HEADS, D_IDX), lambda bi, p, pt: (bi, 0, 0)),
                      pl.BlockSpec((1, N_IDX_HEADS, 1), lambda bi, p, pt: (bi, 0, 0))]
                     + [page_spec(j) for j in range(g)],
            out_specs=pl.BlockSpec((1, 1, g, page), lambda bi, p, pt: (bi, p, 0, 0)),
        ),
        out_shape=jax.ShapeDtypeStruct((b, n_pages // g, g, page), F32),
        compiler_params=_cparams(2),
        name="decode_scores",
    )(page_table.reshape(-1), q_idx.reshape(b, N_IDX_HEADS, D_IDX),
      w_idx.reshape(b, N_IDX_HEADS, 1), *([kidx_t] * g))
    return out.reshape(b, n_pages * page)


def _dec_select_kernel(s_ref, qi_ref, kn_ref, w_ref, mb_ref, mbn_ref, *, kk, idx_bits):
    b, past = s_ref.shape
    n = past + LANES
    qi = qi_ref[...]
    kn = kn_ref[...]
    w = w_ref[...] * (N_IDX_HEADS ** -0.5)
    s_new = jnp.zeros((b, 1), F32)
    for h in range(N_IDX_HEADS):
        d = jnp.sum(qi[:, h * D_IDX:(h + 1) * D_IDX] * kn, axis=1, keepdims=True) * (D_IDX ** -0.5)
        s_new = s_new + w[:, h:h + 1] * jnp.maximum(d, 0.0)
    lane = lax.broadcasted_iota(I32, (1, LANES), 1)
    tail = jnp.where(lane == 0, _sortable(jnp.broadcast_to(s_new, (b, LANES))), KEY_NEG_INF)
    keys = jnp.concatenate([_sortable(s_ref[...]), tail], axis=1)
    pos = lax.broadcasted_iota(I32, (1, n), 1)

    def count(m):
        return jnp.sum(m, axis=1, keepdims=True)

    def value_step(it, t):
        cand = t + lax.shift_left(jnp.int32(1), 31 - it)
        return jnp.where(count(jnp.where(keys >= cand, 1.0, 0.0)) >= kk, cand, t)

    thr = lax.fori_loop(0, 32, value_step, jnp.full((b, 1), INT_MIN, I32))
    need = kk - count(jnp.where(keys > thr, 1.0, 0.0))

    def index_step(it, j):
        cand = j + lax.shift_left(jnp.int32(1), idx_bits - 1 - it)
        c = count(jnp.where(keys == thr, jnp.where(pos < cand, 1.0, 0.0), 0.0))
        return jnp.where(c < need, cand, j)

    jlim = lax.fori_loop(0, idx_bits, index_step, jnp.zeros((b, 1), I32))
    tie = jnp.where(pos <= jlim, 0.0, MASK_NEG)
    sel = jnp.where(keys > thr, 0.0, jnp.where(keys == thr, tie, MASK_NEG))
    sel = jnp.where(keys > KEY_NEG_INF, sel, MASK_NEG)
    mb_ref[...] = sel[:, :past]
    mbn_ref[...] = sel[:, past:]


def _dec_select(scores, q_idx, k_idx_new, w_idx, kk):
    b, past = scores.shape
    kern = functools.partial(_dec_select_kernel, kk=kk, idx_bits=max(1, (past + LANES - 1).bit_length()))
    return pl.pallas_call(
        kern,
        out_shape=[jax.ShapeDtypeStruct((b, past), F32), jax.ShapeDtypeStruct((b, LANES), F32)],
        compiler_params=pltpu.CompilerParams(vmem_limit_bytes=VMEM_LIMIT),
        name="decode_select",
    )(scores, q_idx, k_idx_new, w_idx)


def _dec_attn_kernel(pt_ref, qb_ref, q_ref, kn_ref, vn_ref, mb_ref, mbn_ref, bias_ref, biasn_ref, *refs):
    g = (len(refs) - 4) // 2
    k_refs, v_refs = refs[:g], refs[g:2 * g]
    o_ref, m_s, l_s, acc_s = refs[2 * g:]
    step = pl.program_id(1)
    scale = HEAD_DIM ** -0.5

    @pl.when(step == 0)
    def _():
        m_s[...] = jnp.full(m_s.shape, MASK_NEG, F32)
        l_s[...] = jnp.zeros(l_s.shape, F32)
        acc_s[...] = jnp.zeros(acc_s.shape, F32)

    qb = qb_ref[0]
    s = jnp.concatenate([jnp.sum(kr[0, 0] * qb, axis=1) for kr in k_refs], axis=1)
    s = s * scale + bias_ref[...] + mb_ref[0]
    m_old = m_s[...]
    m_new = jnp.maximum(m_old, jnp.max(s, axis=1, keepdims=True))
    alpha = jnp.exp(m_old - m_new)
    p = jnp.exp(s - m_new)
    l_s[...] = alpha * l_s[...] + jnp.sum(p, axis=1, keepdims=True)
    m_s[...] = m_new
    page = qb.shape[2]
    for h in range(N_HEADS):
        upd = acc_s[h] * alpha[h:h + 1, :]
        for j, vr in enumerate(v_refs):
            upd = upd + p[h:h + 1, j * page:(j + 1) * page] * vr[0, 0, h]
        acc_s[h] = upd

    @pl.when(step == pl.num_programs(1) - 1)
    def _():
        ones = jnp.ones((1, page), F32)
        ctx = jnp.concatenate(
            [lax.dot_general(ones, acc_s[h], (((1,), (1,)), ((), ())), precision=HI,
                             preferred_element_type=F32) for h in range(N_HEADS)], axis=0)
        s_new = (jnp.sum(q_ref[0] * kn_ref[0], axis=1, keepdims=True) * scale
                 + biasn_ref[...] + mbn_ref[0][:, 0:1])
        m_f = jnp.maximum(m_s[...], s_new)
        a = jnp.exp(m_s[...] - m_f)
        pn = jnp.exp(s_new - m_f)
        o_ref[0] = (a * ctx + pn * vn_ref[0]) / (a * l_s[...] + pn)


def _dec_attention(q, k_new, v_new, mb, mb_new, bias, bias_new, ck_t, cv_t, page_table, layer):
    b = q.shape[0]
    n_pages = page_table.shape[1]
    page = ck_t.shape[4]
    g = math.gcd(ATTN_PAGES_PER_STEP, n_pages)
    steps = n_pages // g
    q3 = q.reshape(b, N_HEADS, HEAD_DIM)
    qb = jnp.broadcast_to(q3[:, :, :, None], (b, N_HEADS, HEAD_DIM, page))
    row = lambda: pl.BlockSpec((1, N_HEADS, HEAD_DIM), lambda bi, p, pt: (bi, 0, 0))
    page_spec = lambda j: pl.BlockSpec(
        (1, 1, N_HEADS, HEAD_DIM, page), lambda bi, p, pt: (layer, pt[bi * n_pages + p * g + j], 0, 0, 0))
    return pl.pallas_call(
        _dec_attn_kernel,
        grid_spec=pltpu.PrefetchScalarGridSpec(
            num_scalar_prefetch=1,
            grid=(b, steps),
            in_specs=[pl.BlockSpec((1, N_HEADS, HEAD_DIM, page), lambda bi, p, pt: (bi, 0, 0, 0)),
                      row(), row(), row(),
                      pl.BlockSpec((1, 1, g * page), lambda bi, p, pt: (bi, 0, p)),
                      pl.BlockSpec((1, 1, LANES), lambda bi, p, pt: (bi, 0, 0)),
                      pl.BlockSpec((N_HEADS, g * page), lambda bi, p, pt: (0, p)),
                      pl.BlockSpec((N_HEADS, 1), lambda bi, p, pt: (0, 0))]
                     + [page_spec(j) for j in range(g)] * 2,
            out_specs=row(),
            scratch_shapes=[pltpu.VMEM((N_HEADS, 1), F32), pltpu.VMEM((N_HEADS, 1), F32),
                            pltpu.VMEM((N_HEADS, HEAD_DIM, page), F32)],
        ),
        out_shape=jax.ShapeDtypeStruct((b, N_HEADS, HEAD_DIM), F32),
        compiler_params=_cparams(2),
        name="decode_attention",
    )(page_table.reshape(-1), qb, q3, k_new.reshape(b, N_HEADS, HEAD_DIM),
      v_new.reshape(b, N_HEADS, HEAD_DIM), mb.reshape(b, 1, n_pages * page),
      mb_new.reshape(b, 1, LANES), bias, bias_new, *([ck_t] * g), *([cv_t] * g))


def _decode_bias(rel_bias, past):
    tab = _rel_bucket_table(past + 1)
    bias = _shifted_bias(rel_bias, tab[past - np.arange(past)])
    return bias, (rel_bias[0] - rel_bias[N_BUCKETS - 1]).reshape(N_HEADS, 1)


def kernel(x_prompt, x_sample, cache_k, cache_v, cache_kidx, state_s5_re, state_s5_im, state_lru_h,
           state_lru_conv, page_table, w_in, w_out, ln_g, ln_b, rel_bias, s5_lam_re, s5_lam_im,
           s5_log_step, s5_b_re, s5_b_im, s5_c_re, s5_c_im, s5_d, glu_w, glu_b, lru_conv_w, lru_conv_b,
           lru_wa, lru_ba, lru_wx, lru_bx, lru_lam):
    depth = w_in.shape[0]
    bp, t = x_prompt.shape[0], x_prompt.shape[1]
    bs = x_sample.shape[0]
    n_pages = page_table.shape[1]
    page = cache_k.shape[2]
    past = n_pages * page
    kk_s = min(TOPK_MAX, (past + 1) // 4)
    bands = _prompt_bands(rel_bias)
    dec_bias, dec_bias_new = _decode_bias(rel_bias, past)
    hist = CONV_W - 1
    ck_t = jnp.transpose(cache_k, (0, 1, 3, 4, 2))
    cv_t = jnp.transpose(cache_v, (0, 1, 3, 4, 2))
    kidx_t = jnp.transpose(cache_kidx, (0, 1, 3, 2))
    w_in_t = jnp.transpose(w_in, (2, 0, 1))

    xp = x_prompt.reshape(bp * t, D_MODEL)
    xs = x_sample.reshape(bs, D_MODEL)
    zero_state = jnp.zeros((bp, 1, S5_P), F32)
    zero_h = jnp.zeros((bp, 1, D_LRU), F32)
    zero_conv = jnp.zeros((bp, SUBLANES, D_LRU), F32)
    new_p = [[] for _ in range(7)]
    new_s = [[] for _ in range(7)]

    for l in range(depth):
        w_packed = _pack_w_in_t(w_in_t[:, l, :])
        w_out_l = w_out[l].astype(BF16)
        s5p = _s5_params({'lam_re': s5_lam_re[l], 'lam_im': s5_lam_im[l], 'log_step': s5_log_step[l],
                          'b_re': s5_b_re[l], 'b_im': s5_b_im[l], 'c_re': s5_c_re[l], 'c_im': s5_c_im[l],
                          'd_skip': s5_d[l], 'glu_w': glu_w[l], 'glu_b': glu_b[l]})
        lrup = _lru_params({'conv_w': lru_conv_w[l], 'conv_b': lru_conv_b[l], 'wa': lru_wa[l],
                            'ba': lru_ba[l], 'wx': lru_wx[l], 'bx': lru_bx[l], 'lam': lru_lam[l]})
        lng = ln_g[l].reshape(1, D_MODEL)
        lnb = ln_b[l].reshape(1, D_MODEL)

        hp = _in_proj_prompt(xp.reshape(bp, t, D_MODEL), w_packed)
        o_att = _prompt_attention(hp['qT'], hp['k'], hp['vtc'], hp['qiT'], hp['kw'], hp['kwT'], bands)
        o_att = o_att.reshape(bp * t, D_ATT)
        o_ssm, s5re, s5im = _s5_seq(hp['ssm'], zero_state, zero_state, s5p)
        o_lru, lruh, conv = _lru_seq(hp['lru'], zero_conv, zero_h, lrup)
        flat = lambda a: a.reshape(bp * t, a.shape[-1])
        xp_new = _out_ln(xp, o_att, flat(hp['g']), flat(o_ssm), flat(o_lru), w_out_l, lng, lnb,
                         "out_ln_prompt")
        for lst, a in zip(new_p, (hp['kT'].reshape(bp, N_HEADS, HEAD_DIM, t),
                                  hp['vT'].reshape(bp, N_HEADS, HEAD_DIM, t),
                                  hp['kwT'][:, :D_IDX, :],
                                  s5re.reshape(bp, N_SSM_GROUPS, SSM_STATE),
                                  s5im.reshape(bp, N_SSM_GROUPS, SSM_STATE),
                                  lruh.reshape(bp, D_LRU), conv[:, SUBLANES - hist:, :])):
            lst.append(a)
        xp = xp_new

        hs = _in_proj(xs, w_packed, "in_proj_sample")
        kidx_s = hs['kw'][:, :D_IDX]
        widx_s = hs['kw'][:, D_IDX:D_IDX + N_IDX_HEADS]
        scores = _dec_scores(hs['qi'], widx_s, kidx_t, page_table, l)
        mb, mb_new = _dec_select(scores, hs['qi'], kidx_s, widx_s, kk_s)
        o_att_s = _dec_attention(hs['q'], hs['k'], hs['v'], mb, mb_new, dec_bias, dec_bias_new,
                                 ck_t, cv_t, page_table, l)
        o_ssm_s, s5re_s, s5im_s = _s5_step(hs['ssm'], state_s5_re[l].reshape(bs, S5_P),
                                           state_s5_im[l].reshape(bs, S5_P), s5p)
        conv0 = state_lru_conv[l]
        o_lru_s, lruh_s = _lru_step(hs['lru'], conv0, state_lru_h[l], lrup)
        conv_s = jnp.concatenate([conv0[:, 1:], hs['lru'][:, None, :D_LRU]], axis=1)
        xs_new = _out_ln(xs, o_att_s.reshape(bs, D_ATT), hs['g'], o_ssm_s, o_lru_s, w_out_l, lng, lnb,
                         "out_ln_sample")
        for lst, a in zip(new_s, (hs['k'].reshape(bs, 1, N_HEADS, HEAD_DIM),
                                  hs['v'].reshape(bs, 1, N_HEADS, HEAD_DIM),
                                  kidx_s.reshape(bs, 1, D_IDX),
                                  s5re_s.reshape(bs, N_SSM_GROUPS, SSM_STATE),
                                  s5im_s.reshape(bs, N_SSM_GROUPS, SSM_STATE),
                                  lruh_s, conv_s)):
            lst.append(a)
        xs = xs_new

    outs_p = [jnp.stack(a) for a in new_p]
    outs_p[0] = jnp.transpose(outs_p[0], (0, 1, 4, 2, 3))
    outs_p[1] = jnp.transpose(outs_p[1], (0, 1, 4, 2, 3))
    outs_p[2] = jnp.transpose(outs_p[2], (0, 1, 3, 2))
    outs_s = [jnp.stack(a) for a in new_s]
    return (xp.reshape(bp, t, D_MODEL), xs.reshape(bs, 1, D_MODEL), *outs_p, *outs_s)
```

```python
import functools
import math

import numpy as np
import jax
import jax.numpy as jnp
from jax import lax
from jax.experimental import pallas as pl
from jax.experimental.pallas import tpu as pltpu

F32 = jnp.float32
BF16 = jnp.bfloat16
I32 = jnp.int32
I16 = jnp.int16
I16_MIN = -2 ** 15
I16_MAX = 2 ** 15 - 1
HI = lax.Precision.HIGHEST

D_MODEL = 1024
D_ATT = 512
HEAD_DIM = 64
N_HEADS = 8
N_IDX_HEADS = 4
D_IDX = 64
TOPK_MAX = 256
N_BUCKETS = 32
MAX_DISTANCE = 128
D_SSM = 256
SSM_CH = 16
N_SSM_GROUPS = 16
SSM_STATE = 64
S5_P = N_SSM_GROUPS * SSM_STATE
D_LRU = 256
N_LRU_BLOCKS = 4
CONV_W = 4
LRU_C = 8.0
LN_EPS = 1e-5
DEPTH = 4
DEEPNORM_ALPHA = (2.0 * DEPTH) ** 0.25
LOG2E = math.log2(math.e)

SUBLANES = 8
LANES = 128
VMEM_LIMIT = 56 * 1024 * 1024

QB = 256
KC = 256
KS = 256
N_NEAR = QB // KS + 1
HEAD_GROUP = 4
SCAN_CHUNK = 512
SCORE_PAGES_PER_STEP = 64
ATTN_PAGES_PER_STEP = 32
MASK_NEG = -3e38
INT_MIN = -2 ** 31
KEY_NEG_INF = -2139095041
NO_TIE_LIMIT = 2 ** 30


def _cparams(n_axes, flags=None):
    return pltpu.CompilerParams(dimension_semantics=("arbitrary",) * n_axes,
                                vmem_limit_bytes=VMEM_LIMIT, flags=flags)


def _bdot(a, b):
    return jnp.dot(a.astype(BF16), b.astype(BF16), preferred_element_type=F32)


def _sigmoid(x):
    return 1.0 / (1.0 + jnp.exp(-x))


def _silu(x):
    return x * _sigmoid(x)


def _gelu_tanh(x):
    return 0.5 * x * (1.0 + jnp.tanh(math.sqrt(2.0 / math.pi) * (x + 0.044715 * (x * x * x))))


def _softplus(x):
    return jnp.maximum(x, 0.0) + jnp.log1p(jnp.exp(-jnp.abs(x)))


def _sortable(x):
    bits = pltpu.bitcast(x, I32)
    return bits ^ ((bits >> 31) & 0x7FFFFFFF)


def _rel_bucket_table(n):
    d = np.arange(n)
    exact = N_BUCKETS // 2
    far = exact + (np.log(np.maximum(d, exact).astype(np.float32) / exact)
                   / math.log(MAX_DISTANCE / exact) * (N_BUCKETS - exact)).astype(np.int32)
    return np.where(d < exact, d, np.minimum(far, N_BUCKETS - 1)).astype(np.int32)


IN_WIDTHS = dict(q=D_ATT, k=D_ATT, v=D_ATT, g=D_ATT, qi=N_IDX_HEADS * D_IDX, kw=LANES,
                 ssm=2 * D_SSM, lru=2 * D_LRU)


_offsets = np.concatenate([[0], np.cumsum(list(IN_WIDTHS.values()))]).tolist()
IN_ROWS = {name: (_offsets[i], _offsets[i + 1]) for i, name in enumerate(IN_WIDTHS)}
CONTRACT_LAST = (((1,), (1,)), ((), ()))


def _rows_major(x, wt_ref, name):
    lo, hi = IN_ROWS[name]
    return lax.dot_general(x, wt_ref[lo:hi, :], CONTRACT_LAST, preferred_element_type=F32)


def _feature_major(x, wt_ref, name):
    lo, hi = IN_ROWS[name]
    return lax.dot_general(wt_ref[lo:hi, :], x, CONTRACT_LAST, preferred_element_type=F32)


def _in_proj_kernel(x_ref, wt_ref, *o_refs):
    x = x_ref[...].astype(BF16)
    for name, o_ref in zip(IN_WIDTHS, o_refs):
        o_ref[...] = _rows_major(x, wt_ref, name)


def _pack_w_in_t(w_in_t):
    widths = (D_ATT, D_ATT, D_ATT, D_ATT, N_IDX_HEADS * D_IDX, D_IDX, N_IDX_HEADS,
              D_SSM, D_SSM, D_LRU, D_LRU)
    cut = int(np.sum(widths[:7]))
    pad = jnp.zeros((LANES - D_IDX - N_IDX_HEADS, D_MODEL), w_in_t.dtype)
    return jnp.concatenate([w_in_t[:cut], pad, w_in_t[cut:]], axis=0).astype(BF16)


def _in_proj(x2d, wt_packed, name):
    m = x2d.shape[0]
    tm = min(512, m)
    outs = pl.pallas_call(
        _in_proj_kernel,
        grid=(m // tm,),
        in_specs=[pl.BlockSpec((tm, D_MODEL), lambda i: (i, 0)),
                  pl.BlockSpec(wt_packed.shape, lambda i: (0, 0))],
        out_specs=[pl.BlockSpec((tm, n), lambda i: (i, 0)) for n in IN_WIDTHS.values()],
        out_shape=[jax.ShapeDtypeStruct((m, n), F32) for n in IN_WIDTHS.values()],
        compiler_params=_cparams(1),
        name=name,
    )(x2d, wt_packed)
    return dict(zip(IN_WIDTHS.keys(), outs))


IN_T_GROUPS = ('q', 'k', 'v', 'qi', 'kw')
IN_N_GROUPS = ('k', 'g', 'kw', 'ssm', 'lru')
IN_N_DTYPES = dict(k=BF16, g=F32, kw=BF16, ssm=F32, lru=F32)


def _in_proj_prompt_kernel(x_ref, wt_ref, *o_refs):
    nt_refs = o_refs[:len(IN_T_GROUPS)]
    vtc_ref = o_refs[len(IN_T_GROUPS)]
    nn_refs = o_refs[len(IN_T_GROUPS) + 1:]
    x = x_ref[0].astype(BF16)
    for name, o_ref in zip(IN_T_GROUPS, nt_refs):
        res = _feature_major(x, wt_ref, name)
        o_ref[0] = res
        if name == 'v':
            for j in range(vtc_ref.shape[1]):
                vtc_ref[0, j] = res[:, j * KS:(j + 1) * KS].astype(BF16)
    for name, o_ref in zip(IN_N_GROUPS, nn_refs):
        o_ref[0] = _rows_major(x, wt_ref, name).astype(o_ref.dtype)


def _in_proj_prompt(x, wt_packed):
    b, t = x.shape[0], x.shape[1]
    tm = min(512, t)
    assert t % tm == 0 and tm % KS == 0, (t, tm, KS)
    t_spec = lambda n: pl.BlockSpec((1, n, tm), lambda bi, j: (bi, 0, j))
    n_spec = lambda n: pl.BlockSpec((1, tm, n), lambda bi, j: (bi, j, 0))
    outs = pl.pallas_call(
        _in_proj_prompt_kernel,
        grid=(b, t // tm),
        in_specs=[n_spec(D_MODEL),
                  pl.BlockSpec(wt_packed.shape, lambda bi, j: (0, 0))],
        out_specs=[t_spec(IN_WIDTHS[n]) for n in IN_T_GROUPS]
                  + [pl.BlockSpec((1, tm // KS, D_ATT, KS), lambda bi, j: (bi, j, 0, 0))]
                  + [n_spec(IN_WIDTHS[n]) for n in IN_N_GROUPS],
        out_shape=[jax.ShapeDtypeStruct((b, IN_WIDTHS[n], t), F32) for n in IN_T_GROUPS]
                  + [jax.ShapeDtypeStruct((b, t // KS, D_ATT, KS), BF16)]
                  + [jax.ShapeDtypeStruct((b, t, IN_WIDTHS[n]), IN_N_DTYPES[n]) for n in IN_N_GROUPS],
        compiler_params=_cparams(2),
        name="in_proj_prompt",
    )(x, wt_packed)
    res = {n + 'T': o for n, o in zip(IN_T_GROUPS, outs)}
    res['vtc'] = outs[len(IN_T_GROUPS)]
    res.update(zip(IN_N_GROUPS, outs[len(IN_T_GROUPS) + 1:]))
    return res


def _out_ln_kernel(x_ref, att_ref, gatt_ref, ssm_ref, lru_ref, w_ref, g_ref, b_ref, o_ref):
    att = att_ref[...] * _silu(gatt_ref[...])
    out = _bdot(att, w_ref[0:D_ATT, :])
    out += _bdot(ssm_ref[...], w_ref[D_ATT:D_ATT + D_SSM, :])
    out += _bdot(lru_ref[...], w_ref[D_ATT + D_SSM:, :])
    y = DEEPNORM_ALPHA * x_ref[...] + out
    mu = jnp.mean(y, axis=-1, keepdims=True)
    yc = y - mu
    var = jnp.mean(yc * yc, axis=-1, keepdims=True)
    o_ref[...] = yc * lax.rsqrt(var + LN_EPS) * g_ref[...] + b_ref[...]


def _out_ln(x, att, gatt, ssm, lru, w_out, ln_g, ln_b, name):
    m = x.shape[0]
    tm = min(512, m)
    row = lambda n: pl.BlockSpec((tm, n), lambda i: (i, 0))
    full = lambda a: pl.BlockSpec(a.shape, lambda i: (0,) * a.ndim)
    return pl.pallas_call(
        _out_ln_kernel,
        grid=(m // tm,),
        in_specs=[row(D_MODEL), row(D_ATT), row(D_ATT), row(D_SSM), row(D_LRU),
                  full(w_out), full(ln_g), full(ln_b)],
        out_specs=row(D_MODEL),
        out_shape=jax.ShapeDtypeStruct((m, D_MODEL), F32),
        compiler_params=_cparams(1),
        name=name,
    )(x, att, gatt, ssm, lru, w_out, ln_g, ln_b)


def _prompt_attn_kernel(qt_ref, k_ref, vt_ref, qit_ref, ki_ref, wt_ref, band_ref, o_ref,
                        key_ref, half_ref, mb_ref, m_ref, l_ref, qbd_ref, acc_ref, qcat_ref, *, kk, idx_bits):
    i = pl.program_id(1)
    nck = i + 1
    qit = qit_ref[0].astype(BF16)
    zero_rows = jnp.zeros((LANES - D_IDX, QB), BF16)
    for h in range(N_IDX_HEADS):
        qcat_ref[:, h * QB:(h + 1) * QB] = jnp.concatenate(
            [qit[h * D_IDX:(h + 1) * D_IDX, :], zero_rows], axis=0)
    w = wt_ref[0] * (N_IDX_HEADS ** -0.5 * D_IDX ** -0.5)
    qpos = i * QB + lax.broadcasted_iota(I32, (1, QB), 1)
    row = lax.broadcasted_iota(I32, (KC, 1), 0)

    def score_chunk(c, carry):
        d = jnp.dot(ki_ref[0, c], qcat_ref[...], preferred_element_type=F32)
        s = jnp.zeros((KC, QB), F32)
        for h in range(N_IDX_HEADS):
            s = s + w[h:h + 1, :] * jnp.maximum(d[:, h * QB:(h + 1) * QB], 0.0)
        key = jnp.where(c * KC + row <= qpos, _sortable(s), KEY_NEG_INF)
        key_ref[c] = key
        half_ref[c] = (key >> 16).astype(I16)
        return carry

    lax.fori_loop(0, nck, score_chunk, 0)

    def count(pred):
        def body(c, acc):
            hit = pred(c, key_ref[c])
            return acc + jnp.sum(hit.reshape(KC // SUBLANES, SUBLANES, QB), axis=0)
        acc = lax.fori_loop(0, nck, body, jnp.zeros((SUBLANES, QB), F32))
        return jnp.sum(acc, axis=0, keepdims=True)

    half_tile = 2 * SUBLANES

    def count_half_ge(cand):
        cand16 = cand.astype(I16)

        def body(c, accs):
            hit = jnp.where(half_ref[c] >= cand16, jnp.int16(1), jnp.int16(0))
            hit = hit.reshape(KC // half_tile, half_tile, QB)
            accs = list(accs)
            for r in range(KC // half_tile):
                accs[r % len(accs)] = accs[r % len(accs)] + hit[r]
            return tuple(accs)

        zero = jnp.zeros((half_tile, QB), I16)
        accs = lax.fori_loop(0, nck, body, (zero,) * 4)
        acc = (accs[0] + accs[1]) + (accs[2] + accs[3])
        return jnp.sum(acc.astype(I32), axis=0, keepdims=True)

    def kth_largest_half(rank):
        def step(it, t):
            cand = t + lax.shift_left(jnp.int32(1), 15 - it)
            return jnp.where(count_half_ge(cand) >= rank, cand, t)
        return lax.fori_loop(0, 16, step, jnp.full((1, QB), I16_MIN, I32))

    t_hi = kth_largest_half(kk)
    above = jnp.where(t_hi < I16_MAX, count_half_ge(jnp.minimum(t_hi + 1, I16_MAX)), 0)

    def low_halves(c, carry):
        key = key_ref[c]
        low = (key & 0xFFFF) + I16_MIN
        half_ref[c] = jnp.where((key >> 16) == t_hi, low, I16_MIN).astype(I16)
        return carry

    lax.fori_loop(0, nck, low_halves, 0)
    t_lo = kth_largest_half(kk - above)
    thr = t_hi * 65536 + (t_lo - I16_MIN)
    n_ge = count(lambda c, k: jnp.where(k >= thr, 1.0, 0.0))
    surplus = jnp.max(jnp.where(thr > KEY_NEG_INF, n_ge - kk, 0.0))

    def break_ties():
        need = kk - count(lambda c, k: jnp.where(k > thr, 1.0, 0.0))

        def index_step(it, j):
            cand = j + lax.shift_left(jnp.int32(1), idx_bits - 1 - it)
            cnt = count(lambda c, k: jnp.where(k == thr, jnp.where(c * KC + row < cand, 1.0, 0.0), 0.0))
            return jnp.where(cnt < need, cand, j)
        return lax.fori_loop(0, idx_bits, index_step, jnp.zeros((1, QB), I32))

    jlim = lax.cond(surplus > 0.0, break_ties, lambda: jnp.full((1, QB), NO_TIE_LIMIT, I32))

    def write_mask(c, carry):
        k = key_ref[c]
        tie = jnp.where(c * KC + row <= jlim, 0.0, MASK_NEG)
        sel = jnp.where(k > thr, 0.0, jnp.where(k == thr, tie, MASK_NEG))
        mask = jnp.where(k > KEY_NEG_INF, sel, MASK_NEG)
        mb_ref[pl.ds(c * (KC // KS), KC // KS)] = mask.reshape(KC // KS, KS, QB)
        return carry

    lax.fori_loop(0, nck, write_mask, 0)

    qt = qt_ref[0] * ((HEAD_DIM ** -0.5) * LOG2E)
    zero_blk = jnp.zeros((HEAD_DIM, QB), BF16)
    for g in range(N_HEADS // HEAD_GROUP):
        for j in range(HEAD_GROUP):
            h = g * HEAD_GROUP + j
            qh = qt[h * HEAD_DIM:(h + 1) * HEAD_DIM, :].astype(BF16)
            qbd_ref[g, :, j * QB:(j + 1) * QB] = jnp.concatenate(
                [qh if r == j else zero_blk for r in range(HEAD_GROUP)], axis=0)
    m_ref[...] = jnp.full(m_ref.shape, MASK_NEG, F32)
    l_ref[...] = jnp.zeros(l_ref.shape, F32)
    acc_ref[...] = jnp.zeros((D_ATT, QB), F32)
    gl = HEAD_GROUP * QB
    ones_rows = jnp.ones((2 * SUBLANES, KS), BF16)
    tiles = lambda x: x.reshape(x.shape[0] // SUBLANES, SUBLANES, x.shape[1])
    gw = HEAD_GROUP * HEAD_DIM

    n_steps = (QB // KS) * nck

    def attend(c, band_sel):
        mbc = mb_ref[c]
        mask = jnp.concatenate([mbc] * HEAD_GROUP, axis=1)
        for g in range(N_HEADS // HEAD_GROUP):
            s = jnp.dot(k_ref[0, c, :, g * gw:(g + 1) * gw], qbd_ref[g], preferred_element_type=F32) + mask
            if band_sel is not None:
                s = s + band_ref[band_sel, g]
            s = tiles(s)
            m_old = m_ref[g]
            m_new = jnp.maximum(m_old, jnp.max(jnp.max(s, axis=0), axis=0, keepdims=True))
            alpha = jnp.exp2(m_old - m_new)
            p = jnp.exp2((s - m_new[None]).reshape(KS, gl).astype(BF16))
            m_ref[g] = m_new
            sums = []
            for j in range(HEAD_GROUP):
                h = g * HEAD_GROUP + j
                hs = slice(h * HEAD_DIM, (h + 1) * HEAD_DIM)
                qs = slice(j * QB, (j + 1) * QB)
                pv = jnp.dot(jnp.concatenate([vt_ref[0, c, hs, :], ones_rows], axis=0), p[:, qs],
                             preferred_element_type=F32)
                acc_ref[hs, :] = (alpha[:, qs][None] * tiles(acc_ref[hs, :])
                                  + tiles(pv[:HEAD_DIM])).reshape(HEAD_DIM, QB)
                sums.append(pv[HEAD_DIM:HEAD_DIM + SUBLANES])
            l_ref[g] = alpha * l_ref[g] + jnp.concatenate(sums, axis=1)

    def far_step(c, carry):
        attend(c, None)
        return carry

    n_far = jnp.maximum((QB // KS) * i - 1, 0)

    def near_step(c, carry):
        attend(c, c - (QB // KS) * i + 1)
        return carry

    lax.fori_loop(0, n_far, far_step, 0)
    lax.fori_loop(n_far, n_steps, near_step, 0)
    def normalised(h):
        l = l_ref[h // HEAD_GROUP][:, (h % HEAD_GROUP) * QB:(h % HEAD_GROUP + 1) * QB]
        return (tiles(acc_ref[h * HEAD_DIM:(h + 1) * HEAD_DIM, :]) / l[None]).reshape(HEAD_DIM, QB)

    for hp in range(N_HEADS // 2):
        pair = jnp.concatenate([normalised(2 * hp), normalised(2 * hp + 1)], axis=0)
        o_ref[0, :, hp * LANES:(hp + 1) * LANES] = pair.T


def _shifted_bias(rel_bias, buckets):
    far = rel_bias[N_BUCKETS - 1]
    lead = (N_HEADS,) + (1,) * buckets.ndim
    idx = jnp.asarray(buckets)[None]
    out = jnp.zeros((N_HEADS,) + buckets.shape, F32)
    for b in np.unique(buckets):
        if b != N_BUCKETS - 1:
            out = jnp.where(idx == b, (rel_bias[b] - far).reshape(lead), out)
    return out


def _prompt_bands(rel_bias):
    tab = _rel_bucket_table(QB + KS)
    sj = np.arange(KS)[:, None]
    qi = np.arange(QB)[None, :]
    buckets = np.stack([tab[np.maximum(qi - sj - (n - 1) * KS, 0)] for n in range(N_NEAR)])
    bias = _shifted_bias(rel_bias, buckets) * LOG2E
    bias = bias.reshape(N_HEADS // HEAD_GROUP, HEAD_GROUP, N_NEAR, KS, QB)
    return jnp.transpose(bias, (2, 0, 3, 1, 4)).reshape(N_NEAR, N_HEADS // HEAD_GROUP, KS, HEAD_GROUP * QB)


def _prompt_attention(qt, k, vtc, qit, kw, kwt, bands):
    b, t = qt.shape[0], qt.shape[2]
    assert t % QB == 0 and t % KC == 0 and QB % KS == 0, (t, QB, KC, KS)
    nc = t // KC
    ns = t // KS
    kk = min(TOPK_MAX, t // 4)
    kc = k.reshape(b, ns, KS, D_ATT)
    ki = kw.reshape(b, nc, KC, LANES)
    col = lambda n: pl.BlockSpec((1, n, QB), lambda bi, i: (bi, 0, i))
    res = lambda s: pl.BlockSpec((1,) + s, lambda bi, i: (bi, 0, 0, 0))
    w_rows = pl.BlockSpec((1, SUBLANES, QB), lambda bi, i: (bi, D_IDX // SUBLANES, i))
    kern = functools.partial(_prompt_attn_kernel, kk=kk, idx_bits=max(1, (t - 1).bit_length()))
    return pl.pallas_call(
        kern,
        grid=(b, t // QB),
        in_specs=[col(D_ATT), res((ns, KS, D_ATT)), res((ns, D_ATT, KS)),
                  col(N_IDX_HEADS * D_IDX), res((nc, KC, LANES)), w_rows,
                  pl.BlockSpec(bands.shape, lambda bi, i: (0, 0, 0, 0))],
        out_specs=pl.BlockSpec((1, QB, D_ATT), lambda bi, i: (bi, i, 0)),
        out_shape=jax.ShapeDtypeStruct((b, t, D_ATT), F32),
        scratch_shapes=[pltpu.VMEM((nc, KC, QB), I32), pltpu.VMEM((nc, KC, QB), I16),
                        pltpu.VMEM((ns, KS, QB), F32),
                        pltpu.VMEM((N_HEADS // HEAD_GROUP, SUBLANES, HEAD_GROUP * QB), F32),
                        pltpu.VMEM((N_HEADS // HEAD_GROUP, SUBLANES, HEAD_GROUP * QB), F32),
                        pltpu.VMEM((N_HEADS // HEAD_GROUP, HEAD_GROUP * HEAD_DIM, HEAD_GROUP * QB), BF16),
                        pltpu.VMEM((D_ATT, QB), F32),
                        pltpu.VMEM((LANES, N_IDX_HEADS * QB), BF16)],
        compiler_params=_cparams(2),
        name="prompt_attention",
    )(qt, kc, vtc, qit, ki, kwt, bands)


def _s5_coeffs(lam_re, lam_im, log_step):
    lr = jnp.minimum(lam_re, -1e-4)
    li = lam_im
    dt = jnp.exp(log_step)
    mag = jnp.exp(lr * dt)
    a_re = mag * jnp.cos(li * dt)
    a_im = mag * jnp.sin(li * dt)
    den = lr * lr + li * li
    f_re = ((a_re - 1.0) * lr + a_im * li) / den
    f_im = (a_im * lr - (a_re - 1.0) * li) / den
    return a_re, a_im, f_re, f_im


def _s5_input_drive(u, f_re, f_im, bre_ref, bim_ref):
    w_re = f_re * bre_ref[...] - f_im * bim_ref[...]
    w_im = f_re * bim_ref[...] + f_im * bre_ref[...]
    ub = u.astype(BF16)
    return _bdot(ub, w_re), _bdot(ub, w_im)


def _s5_readout(h_re, h_im, u, gate, cre_ref, cim_ref, d_ref, gw_ref, gb_ref):
    y = _bdot(h_re, cre_ref[...]) - _bdot(h_im, cim_ref[...])
    y = y + d_ref[...] * u
    z = _gelu_tanh(y)
    z = z * _sigmoid(_bdot(z, gw_ref[...]) + gb_ref[...])
    return z * _silu(gate)


def _cmul(ar, ai, br, bi):
    return ar * br - ai * bi, ar * bi + ai * br


def _s5_seq_kernel(x_ref, h0re_ref, h0im_ref, lre_ref, lim_ref, ls_ref, bre_ref, bim_ref,
                   cre_ref, cim_ref, d_ref, gw_ref, gb_ref,
                   o_ref, hre_out, him_out, hre_s, him_s, cre_s, cim_s):
    c = pl.program_id(1)
    tc = x_ref.shape[1]

    @pl.when(c == 0)
    def _():
        cre_s[...] = h0re_ref[0]
        cim_s[...] = h0im_ref[0]

    u = x_ref[0, :, 0:D_SSM]
    gate = x_ref[0, :, D_SSM:]
    a_re, a_im, f_re, f_im = _s5_coeffs(lre_ref[...], lim_ref[...], ls_ref[...])
    bu_re, bu_im = _s5_input_drive(u, f_re, f_im, bre_ref, bim_ref)
    hre_s[...] = bu_re
    him_s[...] = bu_im

    full = lambda x: jnp.broadcast_to(x, (SUBLANES, S5_P))
    pows = [(full(a_re), full(a_im))]
    for _ in range(SUBLANES - 1):
        pows.append(_cmul(pows[-1][0], pows[-1][1], pows[0][0], pows[0][1]))
    row = lax.broadcasted_iota(I32, (SUBLANES, S5_P), 0)
    pw_re, pw_im = pows[SUBLANES - 1]
    for r in range(SUBLANES - 2, -1, -1):
        pw_re = jnp.where(row == r, pows[r][0], pw_re)
        pw_im = jnp.where(row == r, pows[r][1], pw_im)

    shifted = {s: (jnp.where(row >= s, pows[s - 1][0], 0.0), jnp.where(row >= s, pows[s - 1][1], 0.0))
               for s in (1, 2, 4)}

    def tile(j, carry):
        cr, ci = carry
        sl = pl.ds(pl.multiple_of(j * SUBLANES, SUBLANES), SUBLANES)
        xr = hre_s[sl, :]
        xi = him_s[sl, :]
        for s in (1, 2, 4):
            pr, pi = _cmul(shifted[s][0], shifted[s][1], pltpu.roll(xr, s, 0), pltpu.roll(xi, s, 0))
            xr = xr + pr
            xi = xi + pi
        pr, pi = _cmul(pw_re, pw_im, cr, ci)
        xr = xr + pr
        xi = xi + pi
        hre_s[sl, :] = xr
        him_s[sl, :] = xi
        return xr[SUBLANES - 1:, :], xi[SUBLANES - 1:, :]

    cr, ci = lax.fori_loop(0, tc // SUBLANES, tile, (cre_s[...], cim_s[...]))
    cre_s[...] = cr
    cim_s[...] = ci
    hre_out[0] = cr
    him_out[0] = ci
    o_ref[0] = _s5_readout(hre_s[...], him_s[...], u, gate, cre_ref, cim_ref, d_ref, gw_ref, gb_ref)


def _s5_step_kernel(x_ref, h0re_ref, h0im_ref, lre_ref, lim_ref, ls_ref, bre_ref, bim_ref,
                    cre_ref, cim_ref, d_ref, gw_ref, gb_ref, o_ref, hre_out, him_out):
    u = x_ref[:, 0:D_SSM]
    gate = x_ref[:, D_SSM:]
    a_re, a_im, f_re, f_im = _s5_coeffs(lre_ref[...], lim_ref[...], ls_ref[...])
    bu_re, bu_im = _s5_input_drive(u, f_re, f_im, bre_ref, bim_ref)
    pr, pi = _cmul(a_re, a_im, h0re_ref[...], h0im_ref[...])
    h_re = bu_re + pr
    h_im = bu_im + pi
    hre_out[...] = h_re
    him_out[...] = h_im
    o_ref[...] = _s5_readout(h_re, h_im, u, gate, cre_ref, cim_ref, d_ref, gw_ref, gb_ref)


def _s5_params(lp):
    eye = jnp.eye(N_SSM_GROUPS, dtype=F32)

    def in_bd(b):
        return (jnp.transpose(b, (0, 2, 1))[:, :, None, :] * eye[:, None, :, None]).reshape(D_SSM, S5_P)

    def out_bd(c):
        return (jnp.transpose(c, (0, 2, 1))[:, :, None, :] * eye[:, None, :, None]).reshape(S5_P, D_SSM)

    ls = jnp.broadcast_to(lp['log_step'][:, None], (N_SSM_GROUPS, SSM_STATE))
    return (lp['lam_re'].reshape(1, S5_P), lp['lam_im'].reshape(1, S5_P), ls.reshape(1, S5_P),
            in_bd(lp['b_re']), in_bd(lp['b_im']),
            out_bd(lp['c_re']).astype(BF16), out_bd(lp['c_im']).astype(BF16),
            lp['d_skip'].reshape(1, D_SSM), lp['glu_w'].astype(BF16), lp['glu_b'].reshape(1, D_SSM))


def _s5_seq(x, h0_re, h0_im, params):
    b, t = x.shape[0], x.shape[1]
    tc = min(SCAN_CHUNK, t)
    full = lambda a: pl.BlockSpec(a.shape, lambda bi, c: (0,) * a.ndim)
    st = pl.BlockSpec((1, 1, S5_P), lambda bi, c: (bi, 0, 0))
    return pl.pallas_call(
        _s5_seq_kernel,
        grid=(b, t // tc),
        in_specs=[pl.BlockSpec((1, tc, 2 * D_SSM), lambda bi, c: (bi, c, 0)), st, st]
                 + [full(a) for a in params],
        out_specs=[pl.BlockSpec((1, tc, D_SSM), lambda bi, c: (bi, c, 0)), st, st],
        out_shape=[jax.ShapeDtypeStruct((b, t, D_SSM), F32),
                   jax.ShapeDtypeStruct((b, 1, S5_P), F32),
                   jax.ShapeDtypeStruct((b, 1, S5_P), F32)],
        scratch_shapes=[pltpu.VMEM((tc, S5_P), F32), pltpu.VMEM((tc, S5_P), F32),
                        pltpu.VMEM((1, S5_P), F32), pltpu.VMEM((1, S5_P), F32)],
        compiler_params=_cparams(2),
        name="s5_sequence",
    )(x, h0_re, h0_im, *params)


def _s5_step(x, h0_re, h0_im, params):
    b = x.shape[0]
    return pl.pallas_call(
        _s5_step_kernel,
        out_shape=[jax.ShapeDtypeStruct((b, D_SSM), F32),
                   jax.ShapeDtypeStruct((b, S5_P), F32),
                   jax.ShapeDtypeStruct((b, S5_P), F32)],
        compiler_params=pltpu.CompilerParams(vmem_limit_bytes=VMEM_LIMIT),
        name="s5_step",
    )(x, h0_re, h0_im, *params)


def _lru_gates(xc, wa_ref, ba_ref, wx_ref, bx_ref, lam_ref):
    xb = xc.astype(BF16)
    r = _sigmoid(_bdot(xb, wa_ref[...]) + ba_ref[...])
    g = _sigmoid(_bdot(xb, wx_ref[...]) + bx_ref[...])
    log_a = -LRU_C * r * _softplus(-lam_ref[...])
    a = jnp.exp(log_a)
    b = jnp.sqrt(-jnp.tanh(log_a) * (a * a + 1.0)) * (g * xc)
    return a, b


def _lru_seq_kernel(x_ref, conv0_ref, h0_ref, cw_ref, cb_ref, wa_ref, ba_ref, wx_ref, bx_ref, lam_ref,
                    o_ref, h_out, conv_out, ubuf, a_s, b_s, c_s):
    c = pl.program_id(1)
    tc = x_ref.shape[1]
    hist = CONV_W - 1

    @pl.when(c == 0)
    def _():
        ubuf[0:SUBLANES, :] = conv0_ref[0]
        c_s[...] = h0_ref[0]

    u = x_ref[0, :, 0:D_LRU]
    gate = x_ref[0, :, D_LRU:]
    ubuf[SUBLANES:, :] = u
    xc = cb_ref[...] + u * cw_ref[hist:hist + 1, :]
    for j in range(hist):
        xc = xc + ubuf[SUBLANES - hist + j:SUBLANES - hist + j + tc, :] * cw_ref[j:j + 1, :]
    tail = ubuf[tc:tc + SUBLANES, :]
    ubuf[0:SUBLANES, :] = tail
    conv_out[0] = tail

    a, b = _lru_gates(xc, wa_ref, ba_ref, wx_ref, bx_ref, lam_ref)
    a_s[...] = a
    b_s[...] = b
    row = lax.broadcasted_iota(I32, (SUBLANES, D_LRU), 0)

    def tile(j, carry):
        sl = pl.ds(pl.multiple_of(j * SUBLANES, SUBLANES), SUBLANES)
        av = a_s[sl, :]
        bv = b_s[sl, :]
        for s in (1, 2, 4):
            a_sh = jnp.where(row >= s, pltpu.roll(av, s, 0), 1.0)
            b_sh = jnp.where(row >= s, pltpu.roll(bv, s, 0), 0.0)
            bv = av * b_sh + bv
            av = av * a_sh
        h = bv + av * carry
        b_s[sl, :] = h
        return h[SUBLANES - 1:, :]

    carry = lax.fori_loop(0, tc // SUBLANES, tile, c_s[...])
    c_s[...] = carry
    h_out[0] = carry
    o_ref[0] = b_s[...] * _silu(gate)


def _lru_step_kernel(x_ref, c0_ref, c1_ref, c2_ref, h0_ref, cw_ref, cb_ref, wa_ref, ba_ref, wx_ref,
                     bx_ref, lam_ref, o_ref, h_out):
    u = x_ref[:, 0:D_LRU]
    gate = x_ref[:, D_LRU:]
    xc = (cb_ref[...] + c0_ref[...] * cw_ref[0:1, :] + c1_ref[...] * cw_ref[1:2, :]
          + c2_ref[...] * cw_ref[2:3, :] + u * cw_ref[3:4, :])
    a, b = _lru_gates(xc, wa_ref, ba_ref, wx_ref, bx_ref, lam_ref)
    h = a * h0_ref[...] + b
    h_out[...] = h
    o_ref[...] = h * _silu(gate)


def _lru_params(lp):
    eye = jnp.eye(N_LRU_BLOCKS, dtype=F32)
    bd = lambda w: (w[:, :, None, :] * eye[:, None, :, None]).reshape(D_LRU, D_LRU).astype(BF16)
    cw = jnp.concatenate([lp['conv_w'], jnp.zeros((SUBLANES - CONV_W, D_LRU), F32)], axis=0)
    return (cw, lp['conv_b'].reshape(1, D_LRU), bd(lp['wa']), lp['ba'].reshape(1, D_LRU),
            bd(lp['wx']), lp['bx'].reshape(1, D_LRU), lp['lam'].reshape(1, D_LRU))


def _lru_seq(x, conv0, h0, params):
    b, t = x.shape[0], x.shape[1]
    tc = min(SCAN_CHUNK, t)
    full = lambda a: pl.BlockSpec(a.shape, lambda bi, c: (0,) * a.ndim)
    return pl.pallas_call(
        _lru_seq_kernel,
        grid=(b, t // tc),
        in_specs=[pl.BlockSpec((1, tc, 2 * D_LRU), lambda bi, c: (bi, c, 0)),
                  pl.BlockSpec((1, SUBLANES, D_LRU), lambda bi, c: (bi, 0, 0)),
                  pl.BlockSpec((1, 1, D_LRU), lambda bi, c: (bi, 0, 0))]
                 + [full(a) for a in params],
        out_specs=[pl.BlockSpec((1, tc, D_LRU), lambda bi, c: (bi, c, 0)),
                   pl.BlockSpec((1, 1, D_LRU), lambda bi, c: (bi, 0, 0)),
                   pl.BlockSpec((1, SUBLANES, D_LRU), lambda bi, c: (bi, 0, 0))],
        out_shape=[jax.ShapeDtypeStruct((b, t, D_LRU), F32),
                   jax.ShapeDtypeStruct((b, 1, D_LRU), F32),
                   jax.ShapeDtypeStruct((b, SUBLANES, D_LRU), F32)],
        scratch_shapes=[pltpu.VMEM((tc + SUBLANES, D_LRU), F32), pltpu.VMEM((tc, D_LRU), F32),
                        pltpu.VMEM((tc, D_LRU), F32), pltpu.VMEM((1, D_LRU), F32)],
        compiler_params=_cparams(2),
        name="lru_sequence",
    )(x, conv0, h0, *params)


def _lru_step(x, conv0, h0, params):
    b = x.shape[0]
    return pl.pallas_call(
        _lru_step_kernel,
        out_shape=[jax.ShapeDtypeStruct((b, D_LRU), F32), jax.ShapeDtypeStruct((b, D_LRU), F32)],
        compiler_params=pltpu.CompilerParams(vmem_limit_bytes=VMEM_LIMIT),
        name="lru_step",
    )(x, conv0[:, 0], conv0[:, 1], conv0[:, 2], h0, *params)


def _dec_score_kernel(pt_ref, qi_ref, w_ref, *refs):
    pages, o_ref = refs[:-1], refs[-1]
    qi = qi_ref[0].astype(BF16)
    w = w_ref[0] * (N_IDX_HEADS ** -0.5)
    for g, page in enumerate(pages):
        d = jnp.dot(qi, page[0, 0].astype(BF16), preferred_element_type=F32) * (D_IDX ** -0.5)
        o_ref[0, 0, g:g + 1, :] = jnp.sum(w * jnp.maximum(d, 0.0), axis=0, keepdims=True)


def _dec_scores(q_idx, w_idx, kidx_t, page_table, layer):
    b = q_idx.shape[0]
    n_pages = page_table.shape[1]
    page = kidx_t.shape[3]
    g = math.gcd(SCORE_PAGES_PER_STEP, n_pages)
    page_spec = lambda j: pl.BlockSpec(
        (1, 1, D_IDX, page), lambda bi, p, pt: (layer, pt[bi * n_pages + p * g + j], 0, 0))
    out = pl.pallas_call(
        _dec_score_kernel,
        grid_spec=pltpu.PrefetchScalarGridSpec(
            num_scalar_prefetch=1,
            grid=(b, n_pages // g),
            in_specs=[pl.BlockSpec((1, N_IDX_HEADS, D_IDX), lambda bi, p, pt: (bi, 0, 0)),
                      pl.BlockSpec((1, N_IDX_HEADS, 1), lambda bi, p, pt: (bi, 0, 0))]
                     + [page_spec(j) for j in range(g)],
            out_specs=pl.BlockSpec((1, 1, g, page), lambda bi, p, pt: (bi, p, 0, 0)),
        ),
        out_shape=jax.ShapeDtypeStruct((b, n_pages // g, g, page), F32),
        compiler_params=_cparams(2),
        name="decode_scores",
    )(page_table.reshape(-1), q_idx.reshape(b, N_IDX_HEADS, D_IDX),
      w_idx.reshape(b, N_IDX_HEADS, 1), *([kidx_t] * g))
    return out.reshape(b, n_pages * page)


def _dec_select_kernel(s_ref, qi_ref, kn_ref, w_ref, mb_ref, mbn_ref, *, kk, idx_bits):
    b, past = s_ref.shape
    n = past + LANES
    qi = qi_ref[...]
    kn = kn_ref[...]
    w = w_ref[...] * (N_IDX_HEADS ** -0.5)
    s_new = jnp.zeros((b, 1), F32)
    for h in range(N_IDX_HEADS):
        d = jnp.sum(qi[:, h * D_IDX:(h + 1) * D_IDX] * kn, axis=1, keepdims=True) * (D_IDX ** -0.5)
        s_new = s_new + w[:, h:h + 1] * jnp.maximum(d, 0.0)
    lane = lax.broadcasted_iota(I32, (1, LANES), 1)
    tail = jnp.where(lane == 0, _sortable(jnp.broadcast_to(s_new, (b, LANES))), KEY_NEG_INF)
    keys = jnp.concatenate([_sortable(s_ref[...]), tail], axis=1)
    pos = lax.broadcasted_iota(I32, (1, n), 1)

    def count(m):
        return jnp.sum(m, axis=1, keepdims=True)

    def value_step(it, t):
        cand = t + lax.shift_left(jnp.int32(1), 31 - it)
        return jnp.where(count(jnp.where(keys >= cand, 1.0, 0.0)) >= kk, cand, t)

    thr = lax.fori_loop(0, 32, value_step, jnp.full((b, 1), INT_MIN, I32))
    need = kk - count(jnp.where(keys > thr, 1.0, 0.0))

    def index_step(it, j):
        cand = j + lax.shift_left(jnp.int32(1), idx_bits - 1 - it)
        c = count(jnp.where(keys == thr, jnp.where(pos < cand, 1.0, 0.0), 0.0))
        return jnp.where(c < need, cand, j)

    jlim = lax.fori_loop(0, idx_bits, index_step, jnp.zeros((b, 1), I32))
    tie = jnp.where(pos <= jlim, 0.0, MASK_NEG)
    sel = jnp.where(keys > thr, 0.0, jnp.where(keys == thr, tie, MASK_NEG))
    sel = jnp.where(keys > KEY_NEG_INF, sel, MASK_NEG)
    mb_ref[...] = sel[:, :past]
    mbn_ref[...] = sel[:, past:]


def _dec_select(scores, q_idx, k_idx_new, w_idx, kk):
    b, past = scores.shape
    kern = functools.partial(_dec_select_kernel, kk=kk, idx_bits=max(1, (past + LANES - 1).bit_length()))
    return pl.pallas_call(
        kern,
        out_shape=[jax.ShapeDtypeStruct((b, past), F32), jax.ShapeDtypeStruct((b, LANES), F32)],
        compiler_params=pltpu.CompilerParams(vmem_limit_bytes=VMEM_LIMIT),
        name="decode_select",
    )(scores, q_idx, k_idx_new, w_idx)


def _dec_attn_kernel(pt_ref, qb_ref, q_ref, kn_ref, vn_ref, mb_ref, mbn_ref, bias_ref, biasn_ref, *refs):
    g = (len(refs) - 4) // 2
    k_refs, v_refs = refs[:g], refs[g:2 * g]
    o_ref, m_s, l_s, acc_s = refs[2 * g:]
    step = pl.program_id(1)
    scale = HEAD_DIM ** -0.5

    @pl.when(step == 0)
    def _():
        m_s[...] = jnp.full(m_s.shape, MASK_NEG, F32)
        l_s[...] = jnp.zeros(l_s.shape, F32)
        acc_s[...] = jnp.zeros(acc_s.shape, F32)

    qb = qb_ref[0]
    s = jnp.concatenate([jnp.sum(kr[0, 0] * qb, axis=1) for kr in k_refs], axis=1)
    s = s * scale + bias_ref[...] + mb_ref[0]
    m_old = m_s[...]
    m_new = jnp.maximum(m_old, jnp.max(s, axis=1, keepdims=True))
    alpha = jnp.exp(m_old - m_new)
    p = jnp.exp(s - m_new)
    l_s[...] = alpha * l_s[...] + jnp.sum(p, axis=1, keepdims=True)
    m_s[...] = m_new
    page = qb.shape[2]
    for h in range(N_HEADS):
        upd = acc_s[h] * alpha[h:h + 1, :]
        for j, vr in enumerate(v_refs):
            upd = upd + p[h:h + 1, j * page:(j + 1) * page] * vr[0, 0, h]
        acc_s[h] = upd

    @pl.when(step == pl.num_programs(1) - 1)
    def _():
        ones = jnp.ones((1, page), F32)
        ctx = jnp.concatenate(
            [lax.dot_general(ones, acc_s[h], (((1,), (1,)), ((), ())), precision=HI,
                             preferred_element_type=F32) for h in range(N_HEADS)], axis=0)
        s_new = (jnp.sum(q_ref[0] * kn_ref[0], axis=1, keepdims=True) * scale
                 + biasn_ref[...] + mbn_ref[0][:, 0:1])
        m_f = jnp.maximum(m_s[...], s_new)
        a = jnp.exp(m_s[...] - m_f)
        pn = jnp.exp(s_new - m_f)
        o_ref[0] = (a * ctx + pn * vn_ref[0]) / (a * l_s[...] + pn)


def _dec_attention(q, k_new, v_new, mb, mb_new, bias, bias_new, ck_t, cv_t, page_table, layer):
    b = q.shape[0]
    n_pages = page_table.shape[1]
    page = ck_t.shape[4]
    g = math.gcd(ATTN_PAGES_PER_STEP, n_pages)
    steps = n_pages // g
    q3 = q.reshape(b, N_HEADS, HEAD_DIM)
    qb = jnp.broadcast_to(q3[:, :, :, None], (b, N_HEADS, HEAD_DIM, page))
    row = lambda: pl.BlockSpec((1, N_HEADS, HEAD_DIM), lambda bi, p, pt: (bi, 0, 0))
    page_spec = lambda j: pl.BlockSpec(
        (1, 1, N_HEADS, HEAD_DIM, page), lambda bi, p, pt: (layer, pt[bi * n_pages + p * g + j], 0, 0, 0))
    return pl.pallas_call(
        _dec_attn_kernel,
        grid_spec=pltpu.PrefetchScalarGridSpec(
            num_scalar_prefetch=1,
            grid=(b, steps),
            in_specs=[pl.BlockSpec((1, N_HEADS, HEAD_DIM, page), lambda bi, p, pt: (bi, 0, 0, 0)),
                      row(), row(), row(),
                      pl.BlockSpec((1, 1, g * page), lambda bi, p, pt: (bi, 0, p)),
                      pl.BlockSpec((1, 1, LANES), lambda bi, p, pt: (bi, 0, 0)),
                      pl.BlockSpec((N_HEADS, g * page), lambda bi, p, pt: (0, p)),
                      pl.BlockSpec((N_HEADS, 1), lambda bi, p, pt: (0, 0))]
                     + [page_spec(j) for j in range(g)] * 2,
            out_specs=row(),
            scratch_shapes=[pltpu.VMEM((N_HEADS, 1), F32), pltpu.VMEM((N_HEADS, 1), F32),
                            pltpu.VMEM((N_HEADS, HEAD_DIM, page), F32)],
        ),
        out_shape=jax.ShapeDtypeStruct((b, N_HEADS, HEAD_DIM), F32),
        compiler_params=_cparams(2),
        name="decode_attention",
    )(page_table.reshape(-1), qb, q3, k_new.reshape(b, N_HEADS, HEAD_DIM),
      v_new.reshape(b, N_HEADS, HEAD_DIM), mb.reshape(b, 1, n_pages * page),
      mb_new.reshape(b, 1, LANES), bias, bias_new, *([ck_t] * g), *([cv_t] * g))


def _decode_bias(rel_bias, past):
    tab = _rel_bucket_table(past + 1)
    bias = _shifted_bias(rel_bias, tab[past - np.arange(past)])
    return bias, (rel_bias[0] - rel_bias[N_BUCKETS - 1]).reshape(N_HEADS, 1)


def kernel(x_prompt, x_sample, cache_k, cache_v, cache_kidx, state_s5_re, state_s5_im, state_lru_h,
           state_lru_conv, page_table, w_in, w_out, ln_g, ln_b, rel_bias, s5_lam_re, s5_lam_im,
           s5_log_step, s5_b_re, s5_b_im, s5_c_re, s5_c_im, s5_d, glu_w, glu_b, lru_conv_w, lru_conv_b,
           lru_wa, lru_ba, lru_wx, lru_bx, lru_lam):
    depth = w_in.shape[0]
    bp, t = x_prompt.shape[0], x_prompt.shape[1]
    bs = x_sample.shape[0]
    n_pages = page_table.shape[1]
    page = cache_k.shape[2]
    past = n_pages * page
    assert page == LANES and x_sample.shape[1] == 1, (page, x_sample.shape)
    assert t % SCAN_CHUNK == 0 or t < SCAN_CHUNK, t
    kk_s = min(TOPK_MAX, (past + 1) // 4)
    bands = _prompt_bands(rel_bias)
    dec_bias, dec_bias_new = _decode_bias(rel_bias, past)
    hist = CONV_W - 1
    ck_t = jnp.transpose(cache_k, (0, 1, 3, 4, 2))
    cv_t = jnp.transpose(cache_v, (0, 1, 3, 4, 2))
    kidx_t = jnp.transpose(cache_kidx, (0, 1, 3, 2))
    w_in_t = jnp.transpose(w_in, (2, 0, 1))

    xp = x_prompt.reshape(bp * t, D_MODEL)
    xs = x_sample.reshape(bs, D_MODEL)
    zero_state = jnp.zeros((bp, 1, S5_P), F32)
    zero_h = jnp.zeros((bp, 1, D_LRU), F32)
    zero_conv = jnp.zeros((bp, SUBLANES, D_LRU), F32)
    new_p = [[] for _ in range(7)]
    new_s = [[] for _ in range(7)]

    for l in range(depth):
        w_packed = _pack_w_in_t(w_in_t[:, l, :])
        w_out_l = w_out[l].astype(BF16)
        s5p = _s5_params({'lam_re': s5_lam_re[l], 'lam_im': s5_lam_im[l], 'log_step': s5_log_step[l],
                          'b_re': s5_b_re[l], 'b_im': s5_b_im[l], 'c_re': s5_c_re[l], 'c_im': s5_c_im[l],
                          'd_skip': s5_d[l], 'glu_w': glu_w[l], 'glu_b': glu_b[l]})
        lrup = _lru_params({'conv_w': lru_conv_w[l], 'conv_b': lru_conv_b[l], 'wa': lru_wa[l],
                            'ba': lru_ba[l], 'wx': lru_wx[l], 'bx': lru_bx[l], 'lam': lru_lam[l]})
        lng = ln_g[l].reshape(1, D_MODEL)
        lnb = ln_b[l].reshape(1, D_MODEL)

        hp = _in_proj_prompt(xp.reshape(bp, t, D_MODEL), w_packed)
        o_att = _prompt_attention(hp['qT'], hp['k'], hp['vtc'], hp['qiT'], hp['kw'], hp['kwT'], bands)
        o_att = o_att.reshape(bp * t, D_ATT)
        o_ssm, s5re, s5im = _s5_seq(hp['ssm'], zero_state, zero_state, s5p)
        o_lru, lruh, conv = _lru_seq(hp['lru'], zero_conv, zero_h, lrup)
        flat = lambda a: a.reshape(bp * t, a.shape[-1])
        xp_new = _out_ln(xp, o_att, flat(hp['g']), flat(o_ssm), flat(o_lru), w_out_l, lng, lnb,
                         "out_ln_prompt")
        for lst, a in zip(new_p, (hp['kT'].reshape(bp, N_HEADS, HEAD_DIM, t),
                                  hp['vT'].reshape(bp, N_HEADS, HEAD_DIM, t),
                                  hp['kwT'][:, :D_IDX, :],
                                  s5re.reshape(bp, N_SSM_GROUPS, SSM_STATE),
                                  s5im.reshape(bp, N_SSM_GROUPS, SSM_STATE),
                                  lruh.reshape(bp, D_LRU), conv[:, SUBLANES - hist:, :])):
            lst.append(a)
        xp = xp_new

        hs = _in_proj(xs, w_packed, "in_proj_sample")
        kidx_s = hs['kw'][:, :D_IDX]
        widx_s = hs['kw'][:, D_IDX:D_IDX + N_IDX_HEADS]
        scores = _dec_scores(hs['qi'], widx_s, kidx_t, page_table, l)
        mb, mb_new = _dec_select(scores, hs['qi'], kidx_s, widx_s, kk_s)
        o_att_s = _dec_attention(hs['q'], hs['k'], hs['v'], mb, mb_new, dec_bias, dec_bias_new,
                                 ck_t, cv_t, page_table, l)
        o_ssm_s, s5re_s, s5im_s = _s5_step(hs['ssm'], state_s5_re[l].reshape(bs, S5_P),
                                           state_s5_im[l].reshape(bs, S5_P), s5p)
        conv0 = state_lru_conv[l]
        o_lru_s, lruh_s = _lru_step(hs['lru'], conv0, state_lru_h[l], lrup)
        conv_s = jnp.concatenate([conv0[:, 1:], hs['lru'][:, None, :D_LRU]], axis=1)
        xs_new = _out_ln(xs, o_att_s.reshape(bs, D_ATT), hs['g'], o_ssm_s, o_lru_s, w_out_l, lng, lnb,
                         "out_ln_sample")
        for lst, a in zip(new_s, (hs['k'].reshape(bs, 1, N_HEADS, HEAD_DIM),
                                  hs['v'].reshape(bs, 1, N_HEADS, HEAD_DIM),
                                  kidx_s.reshape(bs, 1, D_IDX),
                                  s5re_s.reshape(bs, N_SSM_GROUPS, SSM_STATE),
                                  s5im_s.reshape(bs, N_SSM_GROUPS, SSM_STATE),
                                  lruh_s, conv_s)):
            lst.append(a)
        xs = xs_new

    outs_p = [jnp.stack(a) for a in new_p]
    outs_p[0] = jnp.transpose(outs_p[0], (0, 1, 4, 2, 3))
    outs_p[1] = jnp.transpose(outs_p[1], (0, 1, 4, 2, 3))
    outs_p[2] = jnp.transpose(outs_p[2], (0, 1, 3, 2))
    outs_s = [jnp.stack(a) for a in new_s]
    return (xp.reshape(bp, t, D_MODEL), xs.reshape(bs, 1, D_MODEL), *outs_p, *outs_s)
```
